```python
import math
import jax, jax.numpy as jnp
from jax import lax
import numpy as np

D_MODEL = 1024
BATCH = 8
SEQ = 4096
DEPTH = 4

N_MEM = 256
N_MIXERS = 3
HEAD_DIM = 64
MIX_HEADS = 12
MIX_WIDTH = MIX_HEADS * HEAD_DIM
XA_HEADS = 4
XA_WIDTH = XA_HEADS * HEAD_DIM
OUT_IN = MIX_WIDTH + XA_WIDTH
ROPE_THETA = 500000.0
ROT_DIM = HEAD_DIM // 4
LN_EPS = 1e-5
RMS_EPS = 1e-6
DEEPNORM_ALPHA = (2.0 * DEPTH) ** 0.25
DEEPNORM_BETA = (8.0 * DEPTH) ** -0.25

NSA_KV_HEADS = 4
NSA_GROUP = MIX_HEADS // NSA_KV_HEADS
NSA_KV_WIDTH = NSA_KV_HEADS * HEAD_DIM
CMP_LEN = 32
CMP_STRIDE = 16
CMP_HIDDEN = 128
SEL_BLOCK = 64
SEL_TOPK = 16
WINDOW = 512
NSA_QBLOCK = 32
NSA_IN = MIX_WIDTH + 6 * NSA_KV_WIDTH + 3 * MIX_HEADS + XA_WIDTH

MLA_Q_RANK = 256
MLA_KV_RANK = 128
MLA_NOPE = 64
MLA_ROPE = 32
MLA_V = 64
MLA_THETA = 10000.0
MLA_QBLOCK = 128
MLA_IN = MLA_Q_RANK + MLA_KV_RANK + MLA_ROPE + XA_WIDTH

CONV_CH = MIX_WIDTH
CONV_WIDTH = 31
CONV_IN = 2 * CONV_CH + XA_WIDTH

N_GROUPS = 4
EXPERTS_PER_GROUP = 4
N_EXPERTS = N_GROUPS * EXPERTS_PER_GROUP
EXPERT_FF = 512
TOPK_IN_GROUP = 2

kernel_name = 'hybrid_nsa_mla_conformer_hmoe_trunk'


def layer_norm(x, g, b):
    xf = x.astype(jnp.float32)
    mu = jnp.mean(xf, -1, keepdims=True)
    var = jnp.mean(jnp.square(xf - mu), -1, keepdims=True)
    y = (xf - mu) * lax.rsqrt(var + LN_EPS) * g.astype(jnp.float32) + b.astype(jnp.float32)
    return y.astype(x.dtype)


def rms_norm(x, g):
    xf = x.astype(jnp.float32)
    y = xf * lax.rsqrt(jnp.mean(xf * xf, -1, keepdims=True) + RMS_EPS) * g.astype(jnp.float32)
    return y.astype(x.dtype)


def rope_tables(positions, dim, theta):
    inv = theta ** (-jnp.arange(0, dim, 2, dtype=jnp.float32) / dim)
    ang = positions.astype(jnp.float32)[..., None] * inv
    return jnp.cos(ang), jnp.sin(ang)


def apply_rope(x, cos, sin):
    r = cos.shape[-1]
    x1, x2 = x[..., :r], x[..., r:]
    c, s = cos[:, :, None, :], sin[:, :, None, :]
    return jnp.concatenate([x1 * c - x2 * s, x1 * s + x2 * c], -1).astype(x.dtype)


def partial_rope(x, cos, sin):
    return jnp.concatenate([apply_rope(x[..., :ROT_DIM], cos, sin), x[..., ROT_DIM:]], -1)


def masked_softmax(s, mask):
    s = jnp.where(mask, s.astype(jnp.float32), -jnp.inf)
    m = jnp.max(s, -1, keepdims=True)
    m = jnp.where(jnp.isfinite(m), m, 0.0)
    e = jnp.where(mask, jnp.exp(s - m), 0.0)
    return e / jnp.maximum(jnp.sum(e, -1, keepdims=True), 1e-30)


def nsa_mixer(mix_cols, cmp_pe, cmp_w1, cmp_w2, cos_n, sin_n):
    B, S, _ = mix_cols.shape
    H, K, G, Dh = MIX_HEADS, NSA_KV_HEADS, NSA_GROUP, HEAD_DIM
    offs = np.cumsum([MIX_WIDTH] + [NSA_KV_WIDTH] * 6).tolist()
    q, kc, vc, ks, vs, kw, vw, gate_logits = jnp.split(mix_cols, offs, axis=-1)
    dt = mix_cols.dtype
    q = partial_rope(q.reshape(B, S, H, Dh), cos_n, sin_n)
    ks = partial_rope(ks.reshape(B, S, K, Dh), cos_n, sin_n)
    kw = partial_rope(kw.reshape(B, S, K, Dh), cos_n, sin_n)
    vs, vw = vs.reshape(B, S, K, Dh), vw.reshape(B, S, K, Dh)
    kc, vc = kc.reshape(B, S, K, Dh), vc.reshape(B, S, K, Dh)
    gates = jax.nn.sigmoid(gate_logits.astype(jnp.float32)).astype(dt).reshape(B, S, H, 3)

    n_cmp = (S - CMP_LEN) // CMP_STRIDE + 1
    starts = np.arange(n_cmp) * CMP_STRIDE
    tok_idx = starts[:, None] + np.arange(CMP_LEN)[None, :]
    end_idx = starts + CMP_LEN - 1

    def compress(t, pe, w1, w2):
        blk = t[:, tok_idx] + pe[None, None, :, None, :]
        blk = blk.transpose(0, 1, 3, 2, 4).reshape(B, n_cmp, K, CMP_LEN * Dh)
        return jax.nn.silu(blk @ w1) @ w2

    k_cmp = compress(kc, cmp_pe[0], cmp_w1[0], cmp_w2[0])
    v_cmp = compress(vc, cmp_pe[1], cmp_w1[1], cmp_w2[1])
    k_cmp = partial_rope(k_cmp, cos_n[:, end_idx], sin_n[:, end_idx])

    n_sel = S // SEL_BLOCK
    top_n = min(SEL_TOPK, n_sel)
    sel_start = np.arange(n_sel) * SEL_BLOCK
    overlap = ((starts[:, None] < sel_start[None, :] + SEL_BLOCK)
               & (starts[:, None] + CMP_LEN > sel_start[None, :])).astype(np.float32)
    overlap = jnp.asarray(overlap)
    ksb = ks.reshape(B, n_sel, SEL_BLOCK, K, Dh).transpose(0, 3, 1, 2, 4)
    vsb = vs.reshape(B, n_sel, SEL_BLOCK, K, Dh).transpose(0, 3, 1, 2, 4)
    kwp = jnp.pad(kw, ((0, 0), (WINDOW, 0), (0, 0), (0, 0)))
    vwp = jnp.pad(vw, ((0, 0), (WINDOW, 0), (0, 0), (0, 0)))
    qg = q.reshape(B, S, K, G, Dh)
    scale = 1.0 / math.sqrt(Dh)
    C = NSA_QBLOCK
    bi = jnp.arange(B)[:, None, None, None]
    ki = jnp.arange(K)[None, :, None, None]
    blk_id = jnp.arange(n_sel)

    def block(i):
        t0 = i * C
        qb = lax.dynamic_slice_in_dim(qg, t0, C, axis=1)
        t = t0 + jnp.arange(C)
        s_c = jnp.einsum('bckgd,bnkd->bkgcn', qb, k_cmp) * scale
        p_c = masked_softmax(s_c, end_idx[None, :] <= t[:, None])
        o_c = jnp.einsum('bkgcn,bnkd->bckgd', p_c.astype(dt), v_cmp)
        imp = jnp.einsum('bkgcn,nj->bkcj', p_c, overlap)
        cur = t // SEL_BLOCK
        valid = blk_id[None, :] <= cur[:, None]
        forced = (blk_id[None, :] == 0) | (blk_id[None, :] == cur[:, None]) | (blk_id[None, :] == cur[:, None] - 1)
        imp = jnp.where(valid & forced, jnp.inf, jnp.where(valid, imp, -jnp.inf))
        _, sel = lax.top_k(imp, top_n)
        k_sel = ksb[bi, ki, sel]
        v_sel = vsb[bi, ki, sel]
        s_s = jnp.einsum('bckgd,bkcnld->bkgcnl', qb, k_sel) * scale
        key_pos = sel[..., None] * SEL_BLOCK + jnp.arange(SEL_BLOCK)
        mask_s = (key_pos <= t[None, None, :, None, None]).reshape(B, K, 1, C, top_n * SEL_BLOCK)
        p_s = masked_softmax(s_s.reshape(B, K, G, C, top_n * SEL_BLOCK), mask_s)
        o_s = jnp.einsum('bkgcm,bkcmd->bckgd', p_s.astype(dt),
                         v_sel.reshape(B, K, C, top_n * SEL_BLOCK, Dh))
        kwb = lax.dynamic_slice_in_dim(kwp, t0, WINDOW + C, axis=1)
        vwb = lax.dynamic_slice_in_dim(vwp, t0, WINDOW + C, axis=1)
        s_w = jnp.einsum('bckgd,bjkd->bkgcj', qb, kwb) * scale
        kp = t0 - WINDOW + jnp.arange(WINDOW + C)
        diff = t[:, None] - kp[None, :]
        p_w = masked_softmax(s_w, (diff >= 0) & (diff < WINDOW) & (kp[None, :] >= 0))
        o_w = jnp.einsum('bkgcj,bjkd->bckgd', p_w.astype(dt), vwb)
        gb = lax.dynamic_slice_in_dim(gates, t0, C, axis=1).reshape(B, C, K, G, 3)
        o = gb[..., 0:1] * o_c + gb[..., 1:2] * o_s + gb[..., 2:3] * o_w
        return o.reshape(B, C, H * Dh)

    out = lax.map(block, jnp.arange(S // C))
    return out.transpose(1, 0, 2, 3).reshape(B, S, MIX_WIDTH)


def mla_mixer(mix_cols, q_norm, w_uq, kv_norm, w_ukv, cos_m, sin_m):
    B, S, _ = mix_cols.shape
    H = MIX_HEADS
    dt = mix_cols.dtype
    c_q = mix_cols[..., :MLA_Q_RANK]
    c_kv = mix_cols[..., MLA_Q_RANK:MLA_Q_RANK + MLA_KV_RANK]
    k_rope = mix_cols[..., MLA_Q_RANK + MLA_KV_RANK:]
    q = (rms_norm(c_q, q_norm) @ w_uq).reshape(B, S, H, MLA_NOPE + MLA_ROPE)
    q = jnp.concatenate([q[..., :MLA_NOPE], apply_rope(q[..., MLA_NOPE:], cos_m, sin_m)], -1)
    kv = (rms_norm(c_kv, kv_norm) @ w_ukv).reshape(B, S, H, MLA_NOPE + MLA_V)
    k_nope, v = kv[..., :MLA_NOPE], kv[..., MLA_NOPE:]
    k_rope = apply_rope(k_rope[:, :, None, :], cos_m, sin_m)
    k = jnp.concatenate([k_nope, jnp.broadcast_to(k_rope, (B, S, H, MLA_ROPE))], -1)
    scale = 1.0 / math.sqrt(MLA_NOPE + MLA_ROPE)
    key_idx = jnp.arange(S)

    def block(i):
        t0 = i * MLA_QBLOCK
        qb = lax.dynamic_slice_in_dim(q, t0, MLA_QBLOCK, axis=1)
        s = jnp.einsum('bqhd,bkhd->bhqk', qb, k) * scale
        mask = (t0 + jnp.arange(MLA_QBLOCK))[:, None] >= key_idx[None, :]
        p = masked_softmax(s, mask)
        return jnp.einsum('bhqk,bkhd->bqhd', p.astype(dt), v).reshape(B, MLA_QBLOCK, H * MLA_V)

    out = lax.map(block, jnp.arange(S // MLA_QBLOCK))
    return out.transpose(1, 0, 2, 3).reshape(B, S, H * MLA_V)


def conv_mixer(mix_cols, b_in, dw_w, dw_b, ln_g, ln_b):
    a = mix_cols + b_in
    u = a[..., :CONV_CH] * jax.nn.sigmoid(a[..., CONV_CH:])
    y = lax.conv_general_dilated(u, dw_w[:, None, :], window_strides=(1,),
                                 padding=[(CONV_WIDTH - 1, 0)],
                                 dimension_numbers=('NWC', 'WIO', 'NWC'),
                                 feature_group_count=CONV_CH) + dw_b
    return jax.nn.silu(layer_norm(y, ln_g, ln_b))


def memory_cross_attention(q, mem_k, mem_v):
    s = jnp.einsum('bshd,bmhd->bhsm', q, mem_k) * (1.0 / math.sqrt(HEAD_DIM))
    p = jax.nn.softmax(s.astype(jnp.float32), -1).astype(q.dtype)
    return jnp.einsum('bhsm,bmhd->bshd', p, mem_v)


def hier_moe(x, w_grp, b_grp, w_exp, b_exp, w_gate, w_up, w_down):
    def per_seq(xs):
        S = xs.shape[0]
        pg = jax.nn.softmax((xs @ w_grp + b_grp).astype(jnp.float32), -1)
        g_sel = jnp.argmax(pg, -1)
        g_prob = jnp.max(pg, -1)
        le = (xs @ w_exp + b_exp).astype(jnp.float32).reshape(S, N_GROUPS, EXPERTS_PER_GROUP)
        le = le[jnp.arange(S), g_sel]
        pe = jax.nn.softmax(le, -1)
        top_p, top_i = lax.top_k(pe, TOPK_IN_GROUP)
        top_p = top_p / jnp.sum(top_p, -1, keepdims=True)
        eid = g_sel[:, None] * EXPERTS_PER_GROUP + top_i
        gate = jnp.sum(jax.nn.one_hot(eid, N_EXPERTS, dtype=jnp.float32)
                       * (g_prob[:, None] * top_p)[..., None], axis=1)
        hdn = jax.nn.silu(jnp.einsum('sd,edf->sef', xs, w_gate)) * jnp.einsum('sd,edf->sef', xs, w_up)
        hdn = hdn * gate[..., None].astype(hdn.dtype)
        return jnp.einsum('sef,efd->sd', hdn, w_down)
    return lax.map(per_seq, x)


def setup_inputs(seed: int = 0) -> dict:
    key = jax.random.key(seed)
    ks = iter(jax.random.split(key, 40))
    nA = len(range(0, DEPTH, N_MIXERS))
    nB = len(range(1, DEPTH, N_MIXERS))
    nC = len(range(2, DEPTH, N_MIXERS))
    f32 = jnp.float32

    def nrm(shape, scale):
        return jax.random.normal(next(ks), shape, f32) * scale

    def gain(shape):
        return 1.0 + nrm(shape, 0.02)

    positions = (jax.random.randint(next(ks), (BATCH, 1), 0, 2048, dtype=jnp.int32)
                 + jnp.arange(SEQ, dtype=jnp.int32)[None, :])
    return {
        'x': nrm((BATCH, SEQ, D_MODEL), 1.0),
        'mem': nrm((BATCH, N_MEM, D_MODEL), 1.0),
        'positions': positions,
        'nsa_w_in': nrm((nA, D_MODEL, NSA_IN), D_MODEL ** -0.5),
        'nsa_cmp_pe': nrm((nA, 2, CMP_LEN, HEAD_DIM), 0.1),
        'nsa_cmp_w1': nrm((nA, 2, CMP_LEN * HEAD_DIM, CMP_HIDDEN), (CMP_LEN * HEAD_DIM) ** -0.5),
        'nsa_cmp_w2': nrm((nA, 2, CMP_HIDDEN, HEAD_DIM), CMP_HIDDEN ** -0.5),
        'mla_w_in': nrm((nB, D_MODEL, MLA_IN), D_MODEL ** -0.5),
        'mla_q_norm': gain((nB, MLA_Q_RANK)),
        'mla_w_uq': nrm((nB, MLA_Q_RANK, MIX_HEADS * (MLA_NOPE + MLA_ROPE)), MLA_Q_RANK ** -0.5),
        'mla_kv_norm': gain((nB, MLA_KV_RANK)),
        'mla_w_ukv': nrm((nB, MLA_KV_RANK, MIX_HEADS * (MLA_NOPE + MLA_V)), MLA_KV_RANK ** -0.5),
        'conv_w_in': nrm((nC, D_MODEL, CONV_IN), D_MODEL ** -0.5),
        'conv_b_in': nrm((nC, 2 * CONV_CH), 0.02),
        'conv_dw_w': nrm((nC, CONV_WIDTH, CONV_CH), CONV_WIDTH ** -0.5),
        'conv_dw_b': nrm((nC, CONV_CH), 0.02),
        'conv_ln_g': gain((nC, CONV_CH)),
        'conv_ln_b': nrm((nC, CONV_CH), 0.02),
        'mem_w_kv': nrm((DEPTH, D_MODEL, 2 * XA_WIDTH), D_MODEL ** -0.5),
        'w_out': nrm((DEPTH, OUT_IN, D_MODEL), OUT_IN ** -0.5 * DEEPNORM_BETA),
        'ln_g': gain((DEPTH, 2, D_MODEL)),
        'ln_b': nrm((DEPTH, 2, D_MODEL), 0.02),
        'moe_w_grp': nrm((DEPTH, D_MODEL, N_GROUPS), D_MODEL ** -0.5),
        'moe_b_grp': nrm((DEPTH, N_GROUPS), 0.01),
        'moe_w_exp': nrm((DEPTH, D_MODEL, N_EXPERTS), D_MODEL ** -0.5),
        'moe_b_exp': nrm((DEPTH, N_EXPERTS), 0.01),
        'moe_w_gate': nrm((DEPTH, N_EXPERTS, D_MODEL, EXPERT_FF), D_MODEL ** -0.5),
        'moe_w_up': nrm((DEPTH, N_EXPERTS, D_MODEL, EXPERT_FF), D_MODEL ** -0.5),
        'moe_w_down': nrm((DEPTH, N_EXPERTS, EXPERT_FF, D_MODEL), EXPERT_FF ** -0.5 * DEEPNORM_BETA),
    }


def reference(x, mem, positions, nsa_w_in, nsa_cmp_pe, nsa_cmp_w1, nsa_cmp_w2,
              mla_w_in, mla_q_norm, mla_w_uq, mla_kv_norm, mla_w_ukv,
              conv_w_in, conv_b_in, conv_dw_w, conv_dw_b, conv_ln_g, conv_ln_b,
              mem_w_kv, w_out, ln_g, ln_b,
              moe_w_grp, moe_b_grp, moe_w_exp, moe_b_exp, moe_w_gate, moe_w_up, moe_w_down):
    B, S, _ = x.shape
    cos_n, sin_n = rope_tables(positions, ROT_DIM, ROPE_THETA)
    cos_m, sin_m = rope_tables(positions, MLA_ROPE, MLA_THETA)
    for i in range(DEPTH):
        kind = i % N_MIXERS
        j = i // N_MIXERS
        if kind == 0:
            proj = x @ nsa_w_in[j]
            mix = nsa_mixer(proj[..., :-XA_WIDTH], nsa_cmp_pe[j], nsa_cmp_w1[j], nsa_cmp_w2[j], cos_n, sin_n)
        elif kind == 1:
            proj = x @ mla_w_in[j]
            mix = mla_mixer(proj[..., :-XA_WIDTH], mla_q_norm[j], mla_w_uq[j],
                            mla_kv_norm[j], mla_w_ukv[j], cos_m, sin_m)
        else:
            proj = x @ conv_w_in[j]
            mix = conv_mixer(proj[..., :-XA_WIDTH], conv_b_in[j], conv_dw_w[j], conv_dw_b[j],
                             conv_ln_g[j], conv_ln_b[j])
        xq = proj[..., -XA_WIDTH:].reshape(B, S, XA_HEADS, HEAD_DIM)
        mkv = mem @ mem_w_kv[i]
        mk = mkv[..., :XA_WIDTH].reshape(B, N_MEM, XA_HEADS, HEAD_DIM)
        mv = mkv[..., XA_WIDTH:].reshape(B, N_MEM, XA_HEADS, HEAD_DIM)
        xa = memory_cross_attention(xq, mk, mv).reshape(B, S, XA_WIDTH)
        y = jnp.concatenate([mix, xa], -1) @ w_out[i]
        x = layer_norm(DEEPNORM_ALPHA * x + y, ln_g[i, 0], ln_b[i, 0])
        f = hier_moe(x, moe_w_grp[i], moe_b_grp[i], moe_w_exp[i], moe_b_exp[i],
                     moe_w_gate[i], moe_w_up[i], moe_w_down[i])
        x = layer_norm(DEEPNORM_ALPHA * x + f, ln_g[i, 1], ln_b[i, 1])
    return x
```

```python
import functools
import math

import numpy as np
import jax
import jax.numpy as jnp
from jax import lax
from jax.experimental import pallas as pl
from jax.experimental.pallas import tpu as pltpu

F32 = jnp.float32
BF16 = jnp.bfloat16

D_MODEL = 1024
N_MIXERS = 3
HEAD_DIM = 64
MIX_HEADS = 12
MIX_WIDTH = MIX_HEADS * HEAD_DIM
XA_HEADS = 4
XA_WIDTH = XA_HEADS * HEAD_DIM
ROPE_THETA = 500000.0
ROT_DIM = HEAD_DIM // 4
LN_EPS = 1e-5
RMS_EPS = 1e-6
NSA_KV_HEADS = 4
NSA_GROUP = MIX_HEADS // NSA_KV_HEADS
NSA_KV_WIDTH = NSA_KV_HEADS * HEAD_DIM
CMP_LEN = 32
CMP_STRIDE = 16
CMP_HIDDEN = 128
SEL_BLOCK = 64
SEL_SHIFT = 6
SEL_TOPK = 16
WINDOW = 512
MLA_Q_RANK = 256
MLA_KV_RANK = 128
MLA_NOPE = 64
MLA_ROPE = 32
MLA_V = 64
MLA_THETA = 10000.0
CONV_CH = MIX_WIDTH
CONV_WIDTH = 31
N_GROUPS = 4
EXPERTS_PER_GROUP = 4
N_EXPERTS = N_GROUPS * EXPERTS_PER_GROUP
EXPERT_FF = 512

LANES = 128
MXU_N = 256
VMEM_LIMIT = 56 * 1024 * 1024
NEG = -1e30
ROUTE_OFF = N_GROUPS
XA_SCALE = 1.0 / math.sqrt(HEAD_DIM)


def _cparams(*sem):
    return pltpu.CompilerParams(dimension_semantics=sem, vmem_limit_bytes=VMEM_LIMIT)


def _dot(a, b):
    return jnp.dot(a, b, preferred_element_type=F32)


def _dot_nt(a, b):
    return lax.dot_general(a, b, (((1,), (1,)), ((), ())), preferred_element_type=F32)


def _layer_norm(z, g, b):
    mu = jnp.mean(z, -1, keepdims=True)
    zc = z - mu
    var = jnp.mean(zc * zc, -1, keepdims=True)
    return zc * lax.rsqrt(var + LN_EPS) * g + b


def _sigmoid(x):
    return 1.0 / (1.0 + jnp.exp(-x))


def _rope(acc, r, c_ref, s1_ref, s2_ref):
    n = acc.shape[1]
    rep = n // LANES
    c = jnp.tile(c_ref[...], (1, rep))
    s1 = jnp.tile(s1_ref[...], (1, rep))
    s2 = jnp.tile(s2_ref[...], (1, rep))
    return acc * c + pltpu.roll(acc, n - r, 1) * s1 + pltpu.roll(acc, r, 1) * s2


def _proj_kernel(*refs, n_out, rms, rope_r, ropes, scales):
    it = iter(refs)
    x_ref = next(it)
    g_ref = next(it) if rms else None
    tabs = (next(it), next(it), next(it)) if rope_r else None
    w_refs = [next(it) for _ in range(n_out)]
    o_refs = [next(it) for _ in range(n_out)]
    xv = x_ref[...]
    if rms:
        xf = xv.astype(F32)
        xf = xf * lax.rsqrt(jnp.mean(xf * xf, -1, keepdims=True) + RMS_EPS) * g_ref[...]
        xb = xf.astype(BF16)
    else:
        xb = xv.astype(BF16)
    for k in range(n_out):
        n = w_refs[k].shape[1]
        step = MXU_N if n % MXU_N == 0 else LANES
        for c0 in range(0, n, step):
            acc = _dot(xb, w_refs[k][:, c0:c0 + step])
            if ropes[k]:
                acc = _rope(acc, rope_r, *tabs)
            if scales[k] != 1.0:
                acc = acc * scales[k]
            o_refs[k][:, c0:c0 + step] = acc.astype(o_refs[k].dtype)


def _proj(x, ws, dtypes, *, tm, rms_gain=None, rope=None, ropes=None, scales=None):
    m, kdim = x.shape
    n_out = len(ws)
    ropes = tuple(ropes) if ropes is not None else (False,) * n_out
    scales = tuple(scales) if scales is not None else (1.0,) * n_out
    ins = [x]
    in_specs = [pl.BlockSpec((tm, kdim), lambda i: (i, 0))]
    if rms_gain is not None:
        ins.append(rms_gain.reshape(1, kdim).astype(F32))
        in_specs.append(pl.BlockSpec((1, kdim), lambda i: (0, 0)))
    rope_r = 0
    if rope is not None:
        rope_r = rope[0]
        for t in rope[1:]:
            ins.append(t)
            in_specs.append(pl.BlockSpec((tm, LANES), lambda i: (i, 0)))
    for w in ws:
        ins.append(w)
        in_specs.append(pl.BlockSpec(w.shape, lambda i: (0, 0)))
    out_shape = [jax.ShapeDtypeStruct((m, w.shape[1]), dt) for w, dt in zip(ws, dtypes)]
    out_specs = [pl.BlockSpec((tm, w.shape[1]), lambda i: (i, 0)) for w in ws]
    kern = functools.partial(_proj_kernel, n_out=n_out, rms=rms_gain is not None, rope_r=rope_r,
                             ropes=ropes, scales=scales)
    return pl.pallas_call(kern, grid=(m // tm,), in_specs=in_specs, out_specs=out_specs,
                          out_shape=out_shape, compiler_params=_cparams("parallel"))(*ins)


def _flash_kernel(q_ref, k_ref, v_ref, o_ref, m_ref, l_ref, acc_ref, *, mode, g, tq, tk):
    i = pl.program_id(1)
    dk = q_ref.shape[-1]
    dv = v_ref.shape[-1]
    q = q_ref[0].reshape(g * tq, dk)
    m_ref[...] = jnp.full(m_ref.shape, NEG, F32)
    l_ref[...] = jnp.zeros(l_ref.shape, F32)
    acc_ref[...] = jnp.zeros(acc_ref.shape, F32)

    def step(j, mask):
        start = pl.multiple_of(j * tk, tk)
        kt = k_ref[0, pl.ds(start, tk), :]
        vt = v_ref[0, pl.ds(start, tk), :]
        s = _dot_nt(q, kt)
        if mask is not None:
            s = jnp.where(mask[None], s.reshape(g, tq, tk), NEG).reshape(g * tq, tk)
        m_prev = m_ref[...]
        m_new = jnp.maximum(m_prev, jnp.max(s, -1, keepdims=True))
        alpha = jnp.exp(m_prev - m_new)
        p = jnp.exp(s - m_new)
        if mask is not None:
            p = jnp.where(mask[None], p.reshape(g, tq, tk), 0.0).reshape(g * tq, tk)
        l_ref[...] = alpha * l_ref[...] + jnp.sum(p, -1, keepdims=True)
        acc_ref[...] = alpha * acc_ref[...] + _dot(p.astype(BF16), vt)
        m_ref[...] = m_new

    if mode == "full":
        for j in range(k_ref.shape[1] // tk):
            step(j, None)
    else:
        row = lax.broadcasted_iota(jnp.int32, (tq, tk), 0)
        col = lax.broadcasted_iota(jnp.int32, (tq, tk), 1)
        step(i, col <= row)
        if mode == "causal":
            def body(j, carry):
                step(j, None)
                return carry
            lax.fori_loop(0, i, body, 0)
        else:
            @pl.when(i > 0)
            def _():
                step(i - 1, col > row)

    o = acc_ref[...] / l_ref[...]
    o_ref[0] = o.reshape(g, tq, dv).astype(o_ref.dtype)


def _flash(q, k, v, *, mode, tq, tk, out_dtype):
    n, g, s, dk = q.shape
    sk, dv = v.shape[1], v.shape[2]
    kern = functools.partial(_flash_kernel, mode=mode, g=g, tq=tq, tk=tk)
    return pl.pallas_call(
        kern, grid=(n, s // tq),
        in_specs=[pl.BlockSpec((1, g, tq, dk), lambda b, i: (b, 0, i, 0)),
                  pl.BlockSpec((1, sk, dk), lambda b, i: (b, 0, 0)),
                  pl.BlockSpec((1, sk, dv), lambda b, i: (b, 0, 0))],
        out_specs=pl.BlockSpec((1, g, tq, dv), lambda b, i: (b, 0, i, 0)),
        out_shape=jax.ShapeDtypeStruct((n, g, s, dv), out_dtype),
        scratch_shapes=[pltpu.VMEM((g * tq, 1), F32), pltpu.VMEM((g * tq, 1), F32),
                        pltpu.VMEM((g * tq, dv), F32)],
        compiler_params=_cparams("parallel", "arbitrary"))(q, k, v)


def _cmp_kernel(x_ref, w1s_ref, pe_ref, w1_ref, w2_ref, c_ref, s1_ref, s2_ref, o_ref, *, rope):
    nch = x_ref.shape[1]
    ab = _dot(x_ref[0], w1s_ref[...])
    pe_term = _dot(pe_ref[...], w1_ref[...])[0:1, :]
    nxt = pltpu.roll(ab[:, CMP_HIDDEN:], nch - 1, 0)
    h = ab[:, :CMP_HIDDEN] + nxt + pe_term
    h = h * _sigmoid(h)
    o = _dot(h.astype(BF16), w2_ref[...])
    if rope:
        o = _rope(o, ROT_DIM // 2, c_ref.at[0], s1_ref.at[0], s2_ref.at[0])
    o_ref[0] = o[:, :HEAD_DIM].astype(o_ref.dtype)


def _compress(tc, pe, w1, w2, tabs, *, rope, kvh):
    n, nch, width = tc.shape
    half = CMP_STRIDE * HEAD_DIM
    w1b = w1.astype(BF16)
    w1s = jnp.concatenate([w1b[:half], w1b[half:]], axis=1)
    pe8 = jnp.zeros((8, CMP_LEN * HEAD_DIM), BF16).at[0].set(pe.reshape(-1).astype(BF16))
    w2p = jnp.zeros((CMP_HIDDEN, LANES), BF16).at[:, :HEAD_DIM].set(w2.astype(BF16))
    kern = functools.partial(_cmp_kernel, rope=rope)
    tab_spec = pl.BlockSpec((1, nch, LANES), lambda i: (i // kvh, 0, 0))
    return pl.pallas_call(
        kern, grid=(n,),
        in_specs=[pl.BlockSpec((1, nch, width), lambda i: (i, 0, 0)),
                  pl.BlockSpec(w1s.shape, lambda i: (0, 0)),
                  pl.BlockSpec(pe8.shape, lambda i: (0, 0)),
                  pl.BlockSpec(w1b.shape, lambda i: (0, 0)),
                  pl.BlockSpec(w2p.shape, lambda i: (0, 0)),
                  tab_spec, tab_spec, tab_spec],
        out_specs=pl.BlockSpec((1, nch, HEAD_DIM), lambda i: (i, 0, 0)),
        out_shape=jax.ShapeDtypeStruct((n, nch, HEAD_DIM), BF16),
        compiler_params=_cparams("parallel"))(tc, w1s, pe8, w1b, w2p, *tabs)


def _cmp_attn_kernel(q_ref, kc_ref, vc_ref, ovl_ref, oc_ref, qa_ref, *, g, tq, n_cmp, scale):
    i = pl.program_id(1)
    nch = kc_ref.shape[1]
    n_sel = ovl_ref.shape[0]
    kc = kc_ref[0]
    vc = vc_ref[0]
    t0 = i * tq
    t_row = t0 + lax.broadcasted_iota(jnp.int32, (tq, nch), 0)
    n_col = lax.broadcasted_iota(jnp.int32, (tq, nch), 1)
    vis = (n_col < n_cmp) & (n_col * CMP_STRIDE + (CMP_LEN - 1) <= t_row)
    t_lane = t0 + lax.broadcasted_iota(jnp.int32, (nch, tq), 1)
    n_sub = lax.broadcasted_iota(jnp.int32, (nch, tq), 0)
    vis_t = (n_sub < n_cmp) & (n_sub * CMP_STRIDE + (CMP_LEN - 1) <= t_lane)
    p_sum_t = jnp.zeros((nch, tq), F32)
    for h in range(g):
        qh = q_ref[0, h]
        s = jnp.where(vis, _dot_nt(qh, kc) * scale, NEG)
        m = jnp.max(s, -1, keepdims=True)
        e = jnp.where(vis, jnp.exp(s - m), 0.0)
        p = e / jnp.maximum(jnp.sum(e, -1, keepdims=True), 1e-30)
        oc_ref[0, h] = _dot(p.astype(BF16), vc).astype(oc_ref.dtype)
        st = jnp.where(vis_t, _dot_nt(kc, qh) * scale, NEG)
        mt = jnp.max(st, 0, keepdims=True)
        et = jnp.where(vis_t, jnp.exp(st - mt), 0.0)
        p_sum_t = p_sum_t + et / jnp.maximum(jnp.sum(et, 0, keepdims=True), 1e-30)
    imp = jnp.dot(ovl_ref[...], p_sum_t, preferred_element_type=F32,
                  precision=lax.Precision.HIGHEST)
    t1 = t0 + lax.broadcasted_iota(jnp.int32, (n_sel, tq), 1)
    blk = lax.broadcasted_iota(jnp.int32, (n_sel, tq), 0)
    cur = jnp.right_shift(t1, SEL_SHIFT)
    valid = blk <= cur
    forced = (blk == 0) | (blk == cur) | (blk == cur - 1)
    imp = jnp.where(valid, jnp.where(forced, jnp.inf, imp), -jnp.inf)
    rank = jnp.zeros((n_sel, tq), F32)
    for r in range(n_sel):
        row = imp[r:r + 1, :]
        rank = rank + jnp.where(blk > r, jnp.where(row >= imp, 1.0, 0.0), jnp.where(row > imp, 1.0, 0.0))
    chosen_t = jnp.where((rank < float(min(SEL_TOPK, n_sel))) & valid, 1.0, 0.0).astype(BF16)
    eye = (lax.broadcasted_iota(jnp.int32, (tq, tq), 0)
           == lax.broadcasted_iota(jnp.int32, (tq, tq), 1))
    chosen = _dot_nt(jnp.where(eye, 1.0, 0.0).astype(BF16), chosen_t)
    bias = (chosen - 1.0) * (-NEG)
    for h in range(g):
        qs = q_ref[0, h].astype(F32) * scale
        qa_ref[0, h] = jnp.concatenate([qs, bias], axis=1).astype(qa_ref.dtype)


def _cmp_attn(q, kc, vc, ovl_t, *, tq, n_cmp):
    n, g, s, d = q.shape
    nch = kc.shape[1]
    n_sel = ovl_t.shape[0]
    kern = functools.partial(_cmp_attn_kernel, g=g, tq=tq, n_cmp=n_cmp, scale=1.0 / math.sqrt(HEAD_DIM))
    return pl.pallas_call(
        kern, grid=(n, s // tq),
        in_specs=[pl.BlockSpec((1, g, tq, d), lambda b, i: (b, 0, i, 0)),
                  pl.BlockSpec((1, nch, d), lambda b, i: (b, 0, 0)),
                  pl.BlockSpec((1, nch, d), lambda b, i: (b, 0, 0)),
                  pl.BlockSpec(ovl_t.shape, lambda b, i: (0, 0))],
        out_specs=[pl.BlockSpec((1, g, tq, d), lambda b, i: (b, 0, i, 0)),
                   pl.BlockSpec((1, g, tq, d + n_sel), lambda b, i: (b, 0, i, 0))],
        out_shape=[jax.ShapeDtypeStruct((n, g, s, d), F32),
                   jax.ShapeDtypeStruct((n, g, s, d + n_sel), BF16)],
        compiler_params=_cparams("parallel", "arbitrary"))(q, kc, vc, ovl_t)


def _combine_kernel(gl_ref, oc_ref, os_ref, ow_ref, o_ref, *, g):
    for h in range(g):
        gates = _sigmoid(gl_ref[0, h])
        o = (gates[:, 0:1] * oc_ref[0, h] + gates[:, 1:2] * os_ref[0, h] + gates[:, 2:3] * ow_ref[0, h])
        o_ref[0, h] = o.astype(o_ref.dtype)


def _combine(gl, oc, osel, ow, *, tq):
    n, g, s, d = oc.shape
    spec = pl.BlockSpec((1, g, tq, d), lambda b, i: (b, 0, i, 0))
    return pl.pallas_call(
        functools.partial(_combine_kernel, g=g), grid=(n, s // tq),
        in_specs=[pl.BlockSpec((1, g, tq, 3), lambda b, i: (b, 0, i, 0)), spec, spec, spec],
        out_specs=spec, out_shape=jax.ShapeDtypeStruct((n, g, s, d), BF16),
        compiler_params=_cparams("parallel", "parallel"))(gl, oc, osel, ow)


CONV_HALO = 32
CONV_ROWS = 64


def _conv_kernel(cur_ref, prev_ref, bin_ref, w_ref, wb_ref, g_ref, b_ref, o_ref, u_ref, *, ts):
    i = pl.program_id(1)
    ch = o_ref.shape[-1]

    def glu(a):
        a = a + bin_ref[...]
        return a[:, :ch] * _sigmoid(a[:, ch:])

    u_ref[CONV_HALO:CONV_HALO + ts, :] = glu(cur_ref[0])

    @pl.when(i == 0)
    def _():
        u_ref[0:CONV_HALO, :] = jnp.zeros((CONV_HALO, ch), F32)

    @pl.when(i > 0)
    def _():
        u_ref[0:CONV_HALO, :] = glu(prev_ref[0])

    base = CONV_HALO - (CONV_WIDTH - 1)
    for r0 in range(0, ts, CONV_ROWS):
        acc = jnp.zeros((CONV_ROWS, ch), F32)
        for k in range(CONV_WIDTH):
            acc = acc + u_ref[base + r0 + k:base + r0 + k + CONV_ROWS, :] * w_ref[k:k + 1, :]
        y = _layer_norm(acc + wb_ref[...], g_ref[...], b_ref[...])
        o_ref[0, r0:r0 + CONV_ROWS, :] = (y * _sigmoid(y)).astype(o_ref.dtype)


def _conv(a, b_in, dw_w, dw_b, ln_g, ln_b, *, ts):
    bsz, s, two_ch = a.shape
    ch = two_ch // 2
    hb = ts // CONV_HALO
    row = lambda v: v.reshape(1, -1).astype(F32)
    return pl.pallas_call(
        functools.partial(_conv_kernel, ts=ts), grid=(bsz, s // ts),
        in_specs=[pl.BlockSpec((1, ts, two_ch), lambda b, i: (b, i, 0)),
                  pl.BlockSpec((1, CONV_HALO, two_ch), lambda b, i: (b, jnp.maximum(i * hb - 1, 0), 0)),
                  pl.BlockSpec((1, two_ch), lambda b, i: (0, 0)),
                  pl.BlockSpec((CONV_WIDTH, ch), lambda b, i: (0, 0)),
                  pl.BlockSpec((1, ch), lambda b, i: (0, 0)),
                  pl.BlockSpec((1, ch), lambda b, i: (0, 0)),
                  pl.BlockSpec((1, ch), lambda b, i: (0, 0))],
        out_specs=pl.BlockSpec((1, ts, ch), lambda b, i: (b, i, 0)),
        out_shape=jax.ShapeDtypeStruct((bsz, s, ch), BF16),
        scratch_shapes=[pltpu.VMEM((CONV_HALO + ts, ch), F32)],
        compiler_params=_cparams("parallel", "arbitrary"))(
            a, a, row(b_in), dw_w.astype(F32), row(dw_b), row(ln_g), row(ln_b))


def _out_kernel(mix_ref, xa_ref, x_ref, wm_ref, wx_ref, g_ref, b_ref, wr_ref, br_ref, x1_ref, gate_ref,
                *, alpha):
    y = _dot(mix_ref[...], wm_ref[...]) + _dot(xa_ref[...], wx_ref[...])
    x1 = _layer_norm(alpha * x_ref[...] + y, g_ref[...], b_ref[...])
    x1_ref[...] = x1
    logits = jnp.dot(x1, wr_ref[...], preferred_element_type=F32,
                     precision=lax.Precision.HIGHEST) + br_ref[...]
    lane = lax.broadcasted_iota(jnp.int32, logits.shape, 1)
    far = jnp.int32(LANES)
    lg = jnp.where(lane < N_GROUPS, logits, -jnp.inf)
    mg = jnp.max(lg, -1, keepdims=True)
    g_prob = 1.0 / jnp.sum(jnp.exp(lg - mg), -1, keepdims=True)
    g_sel = jnp.min(jnp.where(lg == mg, lane, far), -1, keepdims=True)
    lo = ROUTE_OFF + EXPERTS_PER_GROUP * g_sel
    le = jnp.where((lane >= lo) & (lane < lo + EXPERTS_PER_GROUP), logits, -jnp.inf)
    m1 = jnp.max(le, -1, keepdims=True)
    i1 = jnp.min(jnp.where(le == m1, lane, far), -1, keepdims=True)
    le2 = jnp.where(lane == i1, -jnp.inf, le)
    m2 = jnp.max(le2, -1, keepdims=True)
    i2 = jnp.min(jnp.where(le2 == m2, lane, far), -1, keepdims=True)
    e2 = jnp.exp(m2 - m1)
    p1 = g_prob / (1.0 + e2)
    gate_ref[...] = jnp.where(lane == i1, p1, jnp.where(lane == i2, p1 * e2, 0.0))


def _out_proj(mix, xa, x, w_out, ln_g, ln_b, w_route, b_route, *, tm, alpha):
    t, d = x.shape
    wm = w_out[:MIX_WIDTH].astype(BF16)
    wx = w_out[MIX_WIDTH:].astype(BF16)
    row = lambda v: v.reshape(1, -1).astype(F32)
    full = lambda a: pl.BlockSpec(a.shape, lambda i: (0, 0))
    ins = [mix, xa, x, wm, wx, row(ln_g), row(ln_b), w_route, b_route]
    in_specs = [pl.BlockSpec((tm, MIX_WIDTH), lambda i: (i, 0)),
                pl.BlockSpec((tm, XA_WIDTH), lambda i: (i, 0)),
                pl.BlockSpec((tm, d), lambda i: (i, 0))] + [full(a) for a in ins[3:]]
    return pl.pallas_call(
        functools.partial(_out_kernel, alpha=alpha), grid=(t // tm,), in_specs=in_specs,
        out_specs=[pl.BlockSpec((tm, d), lambda i: (i, 0)), pl.BlockSpec((tm, LANES), lambda i: (i, 0))],
        out_shape=[jax.ShapeDtypeStruct((t, d), F32), jax.ShapeDtypeStruct((t, LANES), F32)],
        compiler_params=_cparams("parallel"))(*ins)


def _moe_kernel(x_ref, gate_ref, wg_ref, wu_ref, wd_ref, g_ref, b_ref, o_ref, acc_ref, xb_ref, *, alpha):
    e = pl.program_id(1)

    @pl.when(e == 0)
    def _():
        acc_ref[...] = jnp.zeros(acc_ref.shape, F32)
        xb_ref[...] = x_ref[...].astype(BF16)

    xb = xb_ref[...]
    lane = lax.broadcasted_iota(jnp.int32, gate_ref.shape, 1)
    gcol = jnp.sum(jnp.where(lane == ROUTE_OFF + e, gate_ref[...], 0.0), -1, keepdims=True)
    hg = _dot(xb, wg_ref[0])
    h = hg * _sigmoid(hg) * _dot(xb, wu_ref[0]) * gcol
    acc_ref[...] += _dot(h.astype(BF16), wd_ref[0])

    @pl.when(e == pl.num_programs(1) - 1)
    def _():
        o_ref[...] = _layer_norm(alpha * x_ref[...] + acc_ref[...], g_ref[...], b_ref[...])


def _moe(x1, gate, wg, wu, wd, ln_g, ln_b, *, tm, alpha):
    t, d = x1.shape
    ne, _, ff = wg.shape
    row = lambda v: v.reshape(1, -1).astype(F32)
    return pl.pallas_call(
        functools.partial(_moe_kernel, alpha=alpha), grid=(t // tm, ne),
        in_specs=[pl.BlockSpec((tm, d), lambda i, e: (i, 0)),
                  pl.BlockSpec((tm, LANES), lambda i, e: (i, 0)),
                  pl.BlockSpec((1, d, ff), lambda i, e: (e, 0, 0)),
                  pl.BlockSpec((1, d, ff), lambda i, e: (e, 0, 0)),
                  pl.BlockSpec((1, ff, d), lambda i, e: (e, 0, 0)),
                  pl.BlockSpec((1, d), lambda i, e: (0, 0)),
                  pl.BlockSpec((1, d), lambda i, e: (0, 0))],
        out_specs=pl.BlockSpec((tm, d), lambda i, e: (i, 0)),
        out_shape=jax.ShapeDtypeStruct((t, d), F32),
        scratch_shapes=[pltpu.VMEM((tm, d), F32), pltpu.VMEM((tm, d), BF16)],
        compiler_params=_cparams("parallel", "arbitrary"))(
            x1, gate, wg.astype(BF16), wu.astype(BF16), wd.astype(BF16), row(ln_g), row(ln_b))


def _rope_tables(positions, dim, theta, period):
    b, s = positions.shape
    half = dim // 2
    inv = theta ** (-jnp.arange(0, dim, 2, dtype=F32) / dim)
    ang = positions.astype(F32)[..., None] * inv
    cos, sin = jnp.cos(ang), jnp.sin(ang)
    rest = period - dim
    one = jnp.ones((b, s, rest), F32)
    zero_h = jnp.zeros((b, s, half), F32)
    zero_r = jnp.zeros((b, s, rest), F32)
    c = jnp.concatenate([cos, cos, one], -1)
    s1 = jnp.concatenate([-sin, zero_h, zero_r], -1)
    s2 = jnp.concatenate([zero_h, sin, zero_r], -1)
    rep = LANES // period
    return tuple(jnp.tile(t, (1, 1, rep)) for t in (c, s1, s2))


def _heads(a, b, s, h):
    return a.reshape(b, s, h, HEAD_DIM).transpose(0, 2, 1, 3)


def _nsa_mixer(x2, b, s, w_in, cmp_pe, cmp_w1, cmp_w2, tabs_n, tabs_n3, *, tm, tq):
    kv, g, d = NSA_KV_HEADS, NSA_GROUP, HEAD_DIM
    offs = np.cumsum([0, MIX_WIDTH] + [NSA_KV_WIDTH] * 6 + [3 * MIX_HEADS, XA_WIDTH]).tolist()
    col = lambda k: w_in[:, offs[k]:offs[k + 1]]
    wq, wkc, wvc, wks, wvs, wkw, wvw, wgt, wxq = [col(k) for k in range(9)]
    w_rope = jnp.concatenate([wq, wks, wkw], 1).astype(BF16)
    w_plain = jnp.concatenate([wkc, wvc, wvs, wvw], 1).astype(BF16)
    w_gate = jnp.zeros((D_MODEL, LANES), F32).at[:, :3 * MIX_HEADS].set(wgt).astype(BF16)
    flat = lambda t: t.reshape(b * s, LANES)
    pr, pp, pg, xq = _proj(x2, [w_rope, w_plain, w_gate, wxq.astype(BF16)], [BF16, BF16, F32, BF16], tm=tm,
                           rope=(ROT_DIM // 2,) + tuple(flat(t) for t in tabs_n),
                           ropes=(True, False, False, False), scales=(1.0, 1.0, 1.0, XA_SCALE))
    q = _heads(pr[:, :MIX_WIDTH], b, s, MIX_HEADS).reshape(b * kv, g, s, d)
    ks = _heads(pr[:, MIX_WIDTH:MIX_WIDTH + NSA_KV_WIDTH], b, s, kv).reshape(b * kv, s, d)
    kw = _heads(pr[:, MIX_WIDTH + NSA_KV_WIDTH:], b, s, kv).reshape(b * kv, s, d)
    w4 = NSA_KV_WIDTH
    kc = _heads(pp[:, 0:w4], b, s, kv).reshape(b * kv, s // CMP_STRIDE, CMP_STRIDE * d)
    vc = _heads(pp[:, w4:2 * w4], b, s, kv).reshape(b * kv, s // CMP_STRIDE, CMP_STRIDE * d)
    vs = _heads(pp[:, 2 * w4:3 * w4], b, s, kv).reshape(b * kv, s, d)
    vw = _heads(pp[:, 3 * w4:4 * w4], b, s, kv).reshape(b * kv, s, d)
    gl = pg[:, :3 * MIX_HEADS].reshape(b, s, MIX_HEADS, 3).transpose(0, 2, 1, 3).reshape(b * kv, g, s, 3)

    n_cmp = (s - CMP_LEN) // CMP_STRIDE + 1
    nch = s // CMP_STRIDE
    end_tabs = tuple(t[:, CMP_LEN - 1::CMP_STRIDE][:, :nch] for t in tabs_n3)
    end_tabs = tuple(jnp.pad(t, ((0, 0), (0, nch - t.shape[1]), (0, 0))) for t in end_tabs)
    k_cmp = _compress(kc, cmp_pe[0], cmp_w1[0], cmp_w2[0], end_tabs, rope=True, kvh=kv)
    v_cmp = _compress(vc, cmp_pe[1], cmp_w1[1], cmp_w2[1], end_tabs, rope=False, kvh=kv)

    n_sel = s // SEL_BLOCK
    starts = np.arange(nch) * CMP_STRIDE
    sel_start = np.arange(n_sel) * SEL_BLOCK
    ovl = ((starts[:, None] < sel_start[None, :] + SEL_BLOCK)
           & (starts[:, None] + CMP_LEN > sel_start[None, :])
           & (np.arange(nch)[:, None] < n_cmp)).astype(np.float32)
    o_c, q_aug = _cmp_attn(q, k_cmp, v_cmp, jnp.asarray(ovl.T), tq=tq, n_cmp=n_cmp)

    onehot = (np.arange(s)[:, None] // SEL_BLOCK == np.arange(n_sel)[None, :]).astype(np.float32)
    k_aug = jnp.concatenate([ks, jnp.broadcast_to(jnp.asarray(onehot, BF16), (b * kv, s, n_sel))], -1)
    o_s = _flash(q_aug, k_aug, vs, mode="causal", tq=tq, tk=tq, out_dtype=F32)
    q_scaled = q_aug[..., :d]
    o_w = _flash(q_scaled, kw, vw, mode="window", tq=WINDOW, tk=WINDOW, out_dtype=F32)
    o = _combine(gl, o_c, o_s, o_w, tq=tq)
    mix = o.reshape(b, MIX_HEADS, s, d).transpose(0, 2, 1, 3).reshape(b * s, MIX_WIDTH)
    return mix, xq


def _mla_mixer(x2, b, s, w_in, q_norm, w_uq, kv_norm, w_ukv, tabs_m, *, tm, tq):
    h = MIX_HEADS
    w_c = w_in[:, :MLA_Q_RANK + MLA_KV_RANK].astype(BF16)
    w_kr = jnp.zeros((D_MODEL, LANES), F32).at[:, :MLA_ROPE].set(
        w_in[:, MLA_Q_RANK + MLA_KV_RANK:MLA_Q_RANK + MLA_KV_RANK + MLA_ROPE]).astype(BF16)
    w_xq = w_in[:, -XA_WIDTH:].astype(BF16)
    flat = tuple(t.reshape(b * s, LANES) for t in tabs_m)
    c, kr, xq = _proj(x2, [w_c, w_kr, w_xq], [F32, BF16, BF16], tm=tm,
                      rope=(MLA_ROPE // 2,) + flat, ropes=(False, True, False),
                      scales=(1.0, 1.0, XA_SCALE))
    k_rope = kr[:, :MLA_ROPE]
    wq3 = w_uq.reshape(MLA_Q_RANK, h, MLA_NOPE + MLA_ROPE)
    wq_nope = wq3[:, :, :MLA_NOPE].reshape(MLA_Q_RANK, h * MLA_NOPE).astype(BF16)
    wq_rope = wq3[:, :, MLA_NOPE:].reshape(MLA_Q_RANK, h * MLA_ROPE).astype(BF16)
    sc = 1.0 / math.sqrt(MLA_NOPE + MLA_ROPE)
    qn, qr = _proj(c[:, :MLA_Q_RANK], [wq_nope, wq_rope], [BF16, BF16], tm=tm, rms_gain=q_norm,
                   rope=(MLA_ROPE // 2,) + flat, ropes=(False, True), scales=(sc, sc))
    (kvu,) = _proj(c[:, MLA_Q_RANK:], [w_ukv.astype(BF16)], [BF16], tm=tm, rms_gain=kv_norm)
    kvu = kvu.reshape(b, s, h, MLA_NOPE + MLA_V)
    q = jnp.concatenate([qn.reshape(b, s, h, MLA_NOPE), qr.reshape(b, s, h, MLA_ROPE)], -1)
    k = jnp.concatenate([kvu[..., :MLA_NOPE],
                         jnp.broadcast_to(k_rope.reshape(b, s, 1, MLA_ROPE), (b, s, h, MLA_ROPE))], -1)
    q = q.transpose(0, 2, 1, 3).reshape(b * h, 1, s, MLA_NOPE + MLA_ROPE)
    k = k.transpose(0, 2, 1, 3).reshape(b * h, s, MLA_NOPE + MLA_ROPE)
    v = kvu[..., MLA_NOPE:].transpose(0, 2, 1, 3).reshape(b * h, s, MLA_V)
    o = _flash(q, k, v, mode="causal", tq=tq, tk=tq, out_dtype=BF16)
    mix = o.reshape(b, h, s, MLA_V).transpose(0, 2, 1, 3).reshape(b * s, h * MLA_V)
    return mix, xq


def _conv_mixer(x2, b, s, w_in, b_in, dw_w, dw_b, ln_g, ln_b, *, tm, ts):
    a, xq = _proj(x2, [w_in[:, :2 * CONV_CH].astype(BF16), w_in[:, 2 * CONV_CH:].astype(BF16)],
                  [F32, BF16], tm=tm, scales=(1.0, XA_SCALE))
    mix = _conv(a.reshape(b, s, 2 * CONV_CH), b_in, dw_w, dw_b, ln_g, ln_b, ts=ts)
    return mix.reshape(b * s, CONV_CH), xq


def kernel(x, mem, positions, nsa_w_in, nsa_cmp_pe, nsa_cmp_w1, nsa_cmp_w2, mla_w_in, mla_q_norm, mla_w_uq, mla_kv_norm, mla_w_ukv, conv_w_in, conv_b_in, conv_dw_w, conv_dw_b, conv_ln_g, conv_ln_b, mem_w_kv, w_out, ln_g, ln_b, moe_w_grp, moe_b_grp, moe_w_exp, moe_b_exp, moe_w_gate, moe_w_up, moe_w_down):
    b, s, d = x.shape
    depth = w_out.shape[0]
    n_mem = mem.shape[1]
    alpha = (2.0 * depth) ** 0.25
    tm = 512
    tq = 512
    tabs_n3 = _rope_tables(positions, ROT_DIM, ROPE_THETA, HEAD_DIM)
    tabs_m3 = _rope_tables(positions, MLA_ROPE, MLA_THETA, MLA_ROPE)
    x2 = x.reshape(b * s, d)
    mem2 = mem.reshape(b * n_mem, d)
    for i in range(depth):
        kind, j = i % N_MIXERS, i // N_MIXERS
        if kind == 0:
            mix, xq = _nsa_mixer(x2, b, s, nsa_w_in[j], nsa_cmp_pe[j], nsa_cmp_w1[j], nsa_cmp_w2[j],
                                 tabs_n3, tabs_n3, tm=tm, tq=tq)
        elif kind == 1:
            mix, xq = _mla_mixer(x2, b, s, mla_w_in[j], mla_q_norm[j], mla_w_uq[j], mla_kv_norm[j],
                                 mla_w_ukv[j], tabs_m3, tm=tm, tq=tq)
        else:
            mix, xq = _conv_mixer(x2, b, s, conv_w_in[j], conv_b_in[j], conv_dw_w[j], conv_dw_b[j],
                                  conv_ln_g[j], conv_ln_b[j], tm=tm, ts=tq)
        (mkv,) = _proj(mem2, [mem_w_kv[i].astype(BF16)], [BF16], tm=min(tm, b * n_mem))
        mk = mkv[:, :XA_WIDTH].reshape(b, n_mem, XA_HEADS, HEAD_DIM).transpose(0, 2, 1, 3)
        mv = mkv[:, XA_WIDTH:].reshape(b, n_mem, XA_HEADS, HEAD_DIM).transpose(0, 2, 1, 3)
        xa = _flash(_heads(xq, b, s, XA_HEADS).reshape(b * XA_HEADS, 1, s, HEAD_DIM), mk.reshape(b * XA_HEADS, n_mem, HEAD_DIM),
                    mv.reshape(b * XA_HEADS, n_mem, HEAD_DIM), mode="full", tq=tq, tk=n_mem, out_dtype=BF16)
        xa = xa.reshape(b, XA_HEADS, s, HEAD_DIM).transpose(0, 2, 1, 3).reshape(b * s, XA_WIDTH)
        w_route = jnp.zeros((d, LANES), F32).at[:, :N_GROUPS].set(moe_w_grp[i])
        w_route = w_route.at[:, ROUTE_OFF:ROUTE_OFF + N_EXPERTS].set(moe_w_exp[i])
        b_route = jnp.zeros((1, LANES), F32).at[0, :N_GROUPS].set(moe_b_grp[i])
        b_route = b_route.at[0, ROUTE_OFF:ROUTE_OFF + N_EXPERTS].set(moe_b_exp[i])
        x1, gate = _out_proj(mix, xa, x2, w_out[i], ln_g[i, 0], ln_b[i, 0], w_route, b_route,
                             tm=tm, alpha=alpha)
        x2 = _moe(x1, gate, moe_w_gate[i], moe_w_up[i], moe_w_down[i], ln_g[i, 1], ln_b[i, 1],
                  tm=tm, alpha=alpha)
    return x2.reshape(b, s, d)
```

```python
import functools
import math

import numpy as np
import jax
import jax.numpy as jnp
from jax import lax
from jax.experimental import pallas as pl
from jax.experimental.pallas import tpu as pltpu

F32 = jnp.float32
BF16 = jnp.bfloat16

D_MODEL = 1024
N_MIXERS = 3
HEAD_DIM = 64
MIX_HEADS = 12
MIX_WIDTH = MIX_HEADS * HEAD_DIM
XA_HEADS = 4
XA_WIDTH = XA_HEADS * HEAD_DIM
ROPE_THETA = 500000.0
ROT_DIM = HEAD_DIM // 4
LN_EPS = 1e-5
RMS_EPS = 1e-6
NSA_KV_HEADS = 4
NSA_GROUP = MIX_HEADS // NSA_KV_HEADS
NSA_KV_WIDTH = NSA_KV_HEADS * HEAD_DIM
CMP_LEN = 32
CMP_STRIDE = 16
CMP_HIDDEN = 128
SEL_BLOCK = 64
SEL_SHIFT = 6
SEL_TOPK = 16
WINDOW = 512
MLA_Q_RANK = 256
MLA_KV_RANK = 128
MLA_NOPE = 64
MLA_ROPE = 32
MLA_V = 64
MLA_THETA = 10000.0
CONV_CH = MIX_WIDTH
CONV_WIDTH = 31
N_GROUPS = 4
EXPERTS_PER_GROUP = 4
N_EXPERTS = N_GROUPS * EXPERTS_PER_GROUP
EXPERT_FF = 512

LANES = 128
MXU_N = 256
VMEM_LIMIT = 56 * 1024 * 1024
NEG = -1e30
ROUTE_OFF = N_GROUPS
LOG2E = math.log2(math.e)
XA_SCALE = LOG2E / math.sqrt(HEAD_DIM)
FLASH_TQ = 1024
FLASH_DB = 512


def _cparams(*sem):
    return pltpu.CompilerParams(dimension_semantics=sem, vmem_limit_bytes=VMEM_LIMIT)


def _dot(a, b):
    return jnp.dot(a, b, preferred_element_type=F32)


def _dot_nt(a, b):
    return lax.dot_general(a, b, (((1,), (1,)), ((), ())), preferred_element_type=F32)


def _layer_norm(z, g, b):
    mu = jnp.mean(z, -1, keepdims=True)
    zc = z - mu
    var = jnp.mean(zc * zc, -1, keepdims=True)
    return zc * lax.rsqrt(var + LN_EPS) * g + b


def _sigmoid(x):
    return 1.0 / (1.0 + jnp.exp(-x))


def _rope(acc, r, c_ref, s1_ref, s2_ref):
    n = acc.shape[1]
    rep = n // LANES
    c = jnp.tile(c_ref[...], (1, rep))
    s1 = jnp.tile(s1_ref[...], (1, rep))
    s2 = jnp.tile(s2_ref[...], (1, rep))
    return acc * c + pltpu.roll(acc, n - r, 1) * s1 + pltpu.roll(acc, r, 1) * s2


def _proj_kernel(*refs, n_out, rms, rope_r, ropes, scales):
    it = iter(refs)
    x_ref = next(it)
    g_ref = next(it) if rms else None
    tabs = (next(it), next(it), next(it)) if rope_r else None
    w_refs = [next(it) for _ in range(n_out)]
    o_refs = [next(it) for _ in range(n_out)]
    xv = x_ref[...]
    if rms:
        xf = xv.astype(F32)
        xf = xf * lax.rsqrt(jnp.mean(xf * xf, -1, keepdims=True) + RMS_EPS) * g_ref[...]
        xb = xf.astype(BF16)
    else:
        xb = xv.astype(BF16)
    for k in range(n_out):
        n = w_refs[k].shape[1]
        step = MXU_N if n % MXU_N == 0 else LANES
        for c0 in range(0, n, step):
            acc = _dot(xb, w_refs[k][:, c0:c0 + step])
            if ropes[k]:
                acc = _rope(acc, rope_r, *tabs)
            if scales[k] != 1.0:
                acc = acc * scales[k]
            o_refs[k][:, c0:c0 + step] = acc.astype(o_refs[k].dtype)


def _proj(x, ws, dtypes, *, tm, rms_gain=None, rope=None, ropes=None, scales=None):
    m, kdim = x.shape
    n_out = len(ws)
    ropes = tuple(ropes) if ropes is not None else (False,) * n_out
    scales = tuple(scales) if scales is not None else (1.0,) * n_out
    ins = [x]
    in_specs = [pl.BlockSpec((tm, kdim), lambda i: (i, 0))]
    if rms_gain is not None:
        ins.append(rms_gain.reshape(1, kdim).astype(F32))
        in_specs.append(pl.BlockSpec((1, kdim), lambda i: (0, 0)))
    rope_r = 0
    if rope is not None:
        rope_r = rope[0]
        for t in rope[1:]:
            ins.append(t)
            in_specs.append(pl.BlockSpec((tm, LANES), lambda i: (i, 0)))
    for w in ws:
        ins.append(w)
        in_specs.append(pl.BlockSpec(w.shape, lambda i: (0, 0)))
    out_shape = [jax.ShapeDtypeStruct((m, w.shape[1]), dt) for w, dt in zip(ws, dtypes)]
    out_specs = [pl.BlockSpec((tm, w.shape[1]), lambda i: (i, 0)) for w in ws]
    kern = functools.partial(_proj_kernel, n_out=n_out, rms=rms_gain is not None, rope_r=rope_r,
                             ropes=ropes, scales=scales)
    return pl.pallas_call(kern, grid=(m // tm,), in_specs=in_specs, out_specs=out_specs,
                          out_shape=out_shape, compiler_params=_cparams("parallel"))(*ins)


def _flash_kernel(q_ref, k_ref, v_ref, o_ref, m_ref, acc_ref, *, mode, g, tq, tk, db):
    i = pl.program_id(1)
    dv = o_ref.shape[-1]
    m_ref[...] = jnp.full(m_ref.shape, NEG, F32)
    acc_ref[...] = jnp.zeros(acc_ref.shape, F32)

    def block(h, r0, nr, kstart, nk, kind):
        kt = k_ref[0, pl.ds(kstart, nk), :]
        vt = v_ref[0, pl.ds(kstart, nk), :]
        s = _dot_nt(q_ref[0, h, r0:r0 + nr, :], kt)
        if kind is not None:
            row = lax.broadcasted_iota(jnp.int32, (nr, nk), 0)
            col = lax.broadcasted_iota(jnp.int32, (nr, nk), 1)
            s = jnp.where(col <= row if kind == "diag" else col > row, s, NEG)
        a0 = h * tq + r0
        m_prev = m_ref[a0:a0 + nr, :]
        m_new = jnp.maximum(m_prev, jnp.max(s, -1, keepdims=True))
        alpha = jnp.exp2(m_prev - m_new)
        p = jnp.exp2(s - jnp.tile(m_new, (1, nk // LANES)))
        acc_ref[a0:a0 + nr, :] = alpha * acc_ref[a0:a0 + nr, :] + _dot(p.astype(BF16), vt)
        m_ref[a0:a0 + nr, :] = m_new

    if mode == "full":
        for j in range(k_ref.shape[1] // tk):
            for h in range(g):
                block(h, 0, tq, j * tk, tk, None)
    elif mode == "window":
        t0 = pl.multiple_of(i * tq, tq)
        for h in range(g):
            block(h, 0, tq, t0, tq, "diag")

        @pl.when(i > 0)
        def _():
            for h in range(g):
                block(h, 0, tq, pl.multiple_of(t0 - tq, tq), tq, "prev")
    else:
        t0 = pl.multiple_of(i * tq, tq)
        for a in range(tq // db):
            for h in range(g):
                block(h, a * db, db, pl.multiple_of(t0 + a * db, db), db, "diag")
                for c in range(a):
                    block(h, a * db, db, pl.multiple_of(t0 + c * db, db), db, None)

        def body(j, carry):
            for h in range(g):
                block(h, 0, tq, pl.multiple_of(j * tk, tk), tk, None)
            return carry
        lax.fori_loop(0, i * (tq // tk), body, 0)

    acc = acc_ref[...]
    o = acc / pltpu.roll(acc, dv, 1)
    o_ref[0] = o[:, :dv].reshape(g, tq, dv).astype(o_ref.dtype)


def _flash(q, k, v, *, mode, tq, tk, out_dtype, db=None):
    n, g, s, dk = q.shape
    sk, dv = v.shape[1], v.shape[2]
    assert 2 * dv == LANES
    v_ones = jnp.concatenate([v, jnp.ones_like(v)], -1)
    kern = functools.partial(_flash_kernel, mode=mode, g=g, tq=tq, tk=tk, db=db or tq)
    return pl.pallas_call(
        kern, grid=(n, s // tq),
        in_specs=[pl.BlockSpec((1, g, tq, dk), lambda b, i: (b, 0, i, 0)),
                  pl.BlockSpec((1, sk, dk), lambda b, i: (b, 0, 0)),
                  pl.BlockSpec((1, sk, 2 * dv), lambda b, i: (b, 0, 0))],
        out_specs=pl.BlockSpec((1, g, tq, dv), lambda b, i: (b, 0, i, 0)),
        out_shape=jax.ShapeDtypeStruct((n, g, s, dv), out_dtype),
        scratch_shapes=[pltpu.VMEM((g * tq, LANES), F32), pltpu.VMEM((g * tq, 2 * dv), F32)],
        compiler_params=_cparams("parallel", "arbitrary"))(q, k, v_ones)


def _cmp_kernel(x_ref, w1s_ref, pe_ref, w1_ref, w2_ref, c_ref, s1_ref, s2_ref, o_ref, *, rope):
    nch = x_ref.shape[1]
    ab = _dot(x_ref[0], w1s_ref[...])
    pe_term = _dot(pe_ref[...], w1_ref[...])[0:1, :]
    nxt = pltpu.roll(ab[:, CMP_HIDDEN:], nch - 1, 0)
    h = ab[:, :CMP_HIDDEN] + nxt + pe_term
    h = h * _sigmoid(h)
    o = _dot(h.astype(BF16), w2_ref[...])
    if rope:
        o = _rope(o, ROT_DIM // 2, c_ref.at[0], s1_ref.at[0], s2_ref.at[0])
    o_ref[0] = o[:, :HEAD_DIM].astype(o_ref.dtype)


def _compress(tc, pe, w1, w2, tabs, *, rope, kvh):
    n, nch, width = tc.shape
    half = CMP_STRIDE * HEAD_DIM
    w1b = w1.astype(BF16)
    w1s = jnp.concatenate([w1b[:half], w1b[half:]], axis=1)
    pe8 = jnp.zeros((8, CMP_LEN * HEAD_DIM), BF16).at[0].set(pe.reshape(-1).astype(BF16))
    w2p = jnp.zeros((CMP_HIDDEN, LANES), BF16).at[:, :HEAD_DIM].set(w2.astype(BF16))
    kern = functools.partial(_cmp_kernel, rope=rope)
    tab_spec = pl.BlockSpec((1, nch, LANES), lambda i: (i // kvh, 0, 0))
    return pl.pallas_call(
        kern, grid=(n,),
        in_specs=[pl.BlockSpec((1, nch, width), lambda i: (i, 0, 0)),
                  pl.BlockSpec(w1s.shape, lambda i: (0, 0)),
                  pl.BlockSpec(pe8.shape, lambda i: (0, 0)),
                  pl.BlockSpec(w1b.shape, lambda i: (0, 0)),
                  pl.BlockSpec(w2p.shape, lambda i: (0, 0)),
                  tab_spec, tab_spec, tab_spec],
        out_specs=pl.BlockSpec((1, nch, HEAD_DIM), lambda i: (i, 0, 0)),
        out_shape=jax.ShapeDtypeStruct((n, nch, HEAD_DIM), BF16),
        compiler_params=_cparams("parallel"))(tc, w1s, pe8, w1b, w2p, *tabs)


def _cmp_attn_kernel(q_ref, kc_ref, vc_ref, ovl_ref, oc_ref, qa_ref, *, g, tq, n_cmp, scale):
    i = pl.program_id(1)
    nch = kc_ref.shape[1]
    n_sel = ovl_ref.shape[0]
    kc = kc_ref[0]
    vc = vc_ref[0]
    t0 = i * tq
    t_row = t0 + lax.broadcasted_iota(jnp.int32, (tq, nch), 0)
    n_col = lax.broadcasted_iota(jnp.int32, (tq, nch), 1)
    vis = (n_col < n_cmp) & (n_col * CMP_STRIDE + (CMP_LEN - 1) <= t_row)
    t_lane = t0 + lax.broadcasted_iota(jnp.int32, (nch, tq), 1)
    n_sub = lax.broadcasted_iota(jnp.int32, (nch, tq), 0)
    vis_t = (n_sub < n_cmp) & (n_sub * CMP_STRIDE + (CMP_LEN - 1) <= t_lane)
    p_sum_t = jnp.zeros((nch, tq), F32)
    for h in range(g):
        qh = q_ref[0, h]
        s = jnp.where(vis, _dot_nt(qh, kc) * scale, NEG)
        m = jnp.max(s, -1, keepdims=True)
        e = jnp.where(vis, jnp.exp(s - m), 0.0)
        p = e / jnp.maximum(jnp.sum(e, -1, keepdims=True), 1e-30)
        oc_ref[0, h] = _dot(p.astype(BF16), vc).astype(oc_ref.dtype)
        st = jnp.where(vis_t, _dot_nt(kc, qh) * scale, NEG)
        mt = jnp.max(st, 0, keepdims=True)
        et = jnp.where(vis_t, jnp.exp(st - mt), 0.0)
        p_sum_t = p_sum_t + et / jnp.maximum(jnp.sum(et, 0, keepdims=True), 1e-30)
    imp = jnp.dot(ovl_ref[...], p_sum_t, preferred_element_type=F32,
                  precision=lax.Precision.HIGHEST)
    t1 = t0 + lax.broadcasted_iota(jnp.int32, (n_sel, tq), 1)
    blk = lax.broadcasted_iota(jnp.int32, (n_sel, tq), 0)
    cur = jnp.right_shift(t1, SEL_SHIFT)
    valid = blk <= cur
    forced = (blk == 0) | (blk == cur) | (blk == cur - 1)
    imp = jnp.where(valid, jnp.where(forced, jnp.inf, imp), -jnp.inf)
    rank = jnp.zeros((n_sel, tq), F32)
    for r in range(n_sel):
        row = imp[r:r + 1, :]
        rank = rank + jnp.where(blk > r, jnp.where(row >= imp, 1.0, 0.0), jnp.where(row > imp, 1.0, 0.0))
    chosen_t = jnp.where((rank < float(min(SEL_TOPK, n_sel))) & valid, 1.0, 0.0).astype(BF16)
    eye = (lax.broadcasted_iota(jnp.int32, (tq, tq), 0)
           == lax.broadcasted_iota(jnp.int32, (tq, tq), 1))
    chosen = _dot_nt(jnp.where(eye, 1.0, 0.0).astype(BF16), chosen_t)
    bias = (chosen - 1.0) * (-NEG)
    for h in range(g):
        qs = q_ref[0, h].astype(F32) * (scale * LOG2E)
        qa_ref[0, h] = jnp.concatenate([qs, bias], axis=1).astype(qa_ref.dtype)


def _cmp_attn(q, kc, vc, ovl_t, *, tq, n_cmp):
    n, g, s, d = q.shape
    nch = kc.shape[1]
    n_sel = ovl_t.shape[0]
    kern = functools.partial(_cmp_attn_kernel, g=g, tq=tq, n_cmp=n_cmp, scale=1.0 / math.sqrt(HEAD_DIM))
    return pl.pallas_call(
        kern, grid=(n, s // tq),
        in_specs=[pl.BlockSpec((1, g, tq, d), lambda b, i: (b, 0, i, 0)),
                  pl.BlockSpec((1, nch, d), lambda b, i: (b, 0, 0)),
                  pl.BlockSpec((1, nch, d), lambda b, i: (b, 0, 0)),
                  pl.BlockSpec(ovl_t.shape, lambda b, i: (0, 0))],
        out_specs=[pl.BlockSpec((1, g, tq, d), lambda b, i: (b, 0, i, 0)),
                   pl.BlockSpec((1, g, tq, d + n_sel), lambda b, i: (b, 0, i, 0))],
        out_shape=[jax.ShapeDtypeStruct((n, g, s, d), F32),
                   jax.ShapeDtypeStruct((n, g, s, d + n_sel), BF16)],
        compiler_params=_cparams("parallel", "arbitrary"))(q, kc, vc, ovl_t)


def _combine_kernel(gl_ref, oc_ref, os_ref, ow_ref, o_ref, *, g):
    for h in range(g):
        gates = _sigmoid(gl_ref[0, h])
        o = (gates[:, 0:1] * oc_ref[0, h] + gates[:, 1:2] * os_ref[0, h] + gates[:, 2:3] * ow_ref[0, h])
        o_ref[0, h] = o.astype(o_ref.dtype)


def _combine(gl, oc, osel, ow, *, tq):
    n, g, s, d = oc.shape
    spec = pl.BlockSpec((1, g, tq, d), lambda b, i: (b, 0, i, 0))
    return pl.pallas_call(
        functools.partial(_combine_kernel, g=g), grid=(n, s // tq),
        in_specs=[pl.BlockSpec((1, g, tq, 3), lambda b, i: (b, 0, i, 0)), spec, spec, spec],
        out_specs=spec, out_shape=jax.ShapeDtypeStruct((n, g, s, d), BF16),
        compiler_params=_cparams("parallel", "parallel"))(gl, oc, osel, ow)


CONV_HALO = 32
CONV_ROWS = 64


def _conv_kernel(cur_ref, prev_ref, bin_ref, w_ref, wb_ref, g_ref, b_ref, o_ref, u_ref, *, ts):
    i = pl.program_id(1)
    ch = o_ref.shape[-1]

    def glu(a):
        a = a + bin_ref[...]
        return a[:, :ch] * _sigmoid(a[:, ch:])

    u_ref[CONV_HALO:CONV_HALO + ts, :] = glu(cur_ref[0])

    @pl.when(i == 0)
    def _():
        u_ref[0:CONV_HALO, :] = jnp.zeros((CONV_HALO, ch), F32)

    @pl.when(i > 0)
    def _():
        u_ref[0:CONV_HALO, :] = glu(prev_ref[0])

    base = CONV_HALO - (CONV_WIDTH - 1)
    for r0 in range(0, ts, CONV_ROWS):
        acc = jnp.zeros((CONV_ROWS, ch), F32)
        for k in range(CONV_WIDTH):
            acc = acc + u_ref[base + r0 + k:base + r0 + k + CONV_ROWS, :] * w_ref[k:k + 1, :]
        y = _layer_norm(acc + wb_ref[...], g_ref[...], b_ref[...])
        o_ref[0, r0:r0 + CONV_ROWS, :] = (y * _sigmoid(y)).astype(o_ref.dtype)


def _conv(a, b_in, dw_w, dw_b, ln_g, ln_b, *, ts):
    bsz, s, two_ch = a.shape
    ch = two_ch // 2
    hb = ts // CONV_HALO
    row = lambda v: v.reshape(1, -1).astype(F32)
    return pl.pallas_call(
        functools.partial(_conv_kernel, ts=ts), grid=(bsz, s // ts),
        in_specs=[pl.BlockSpec((1, ts, two_ch), lambda b, i: (b, i, 0)),
                  pl.BlockSpec((1, CONV_HALO, two_ch), lambda b, i: (b, jnp.maximum(i * hb - 1, 0), 0)),
                  pl.BlockSpec((1, two_ch), lambda b, i: (0, 0)),
                  pl.BlockSpec((CONV_WIDTH, ch), lambda b, i: (0, 0)),
                  pl.BlockSpec((1, ch), lambda b, i: (0, 0)),
                  pl.BlockSpec((1, ch), lambda b, i: (0, 0)),
                  pl.BlockSpec((1, ch), lambda b, i: (0, 0))],
        out_specs=pl.BlockSpec((1, ts, ch), lambda b, i: (b, i, 0)),
        out_shape=jax.ShapeDtypeStruct((bsz, s, ch), BF16),
        scratch_shapes=[pltpu.VMEM((CONV_HALO + ts, ch), F32)],
        compiler_params=_cparams("parallel", "arbitrary"))(
            a, a, row(b_in), dw_w.astype(F32), row(dw_b), row(ln_g), row(ln_b))


def _out_kernel(mix_ref, xa_ref, x_ref, wm_ref, wx_ref, g_ref, b_ref, wr_ref, br_ref, x1_ref, gate_ref,
                *, alpha):
    y = _dot(mix_ref[...], wm_ref[...]) + _dot(xa_ref[...], wx_ref[...])
    x1 = _layer_norm(alpha * x_ref[...] + y, g_ref[...], b_ref[...])
    x1_ref[...] = x1
    logits = jnp.dot(x1, wr_ref[...], preferred_element_type=F32,
                     precision=lax.Precision.HIGHEST) + br_ref[...]
    lane = lax.broadcasted_iota(jnp.int32, logits.shape, 1)
    far = jnp.int32(LANES)
    lg = jnp.where(lane < N_GROUPS, logits, -jnp.inf)
    mg = jnp.max(lg, -1, keepdims=True)
    g_prob = 1.0 / jnp.sum(jnp.exp(lg - mg), -1, keepdims=True)
    g_sel = jnp.min(jnp.where(lg == mg, lane, far), -1, keepdims=True)
    lo = ROUTE_OFF + EXPERTS_PER_GROUP * g_sel
    le = jnp.where((lane >= lo) & (lane < lo + EXPERTS_PER_GROUP), logits, -jnp.inf)
    m1 = jnp.max(le, -1, keepdims=True)
    i1 = jnp.min(jnp.where(le == m1, lane, far), -1, keepdims=True)
    le2 = jnp.where(lane == i1, -jnp.inf, le)
    m2 = jnp.max(le2, -1, keepdims=True)
    i2 = jnp.min(jnp.where(le2 == m2, lane, far), -1, keepdims=True)
    e2 = jnp.exp(m2 - m1)
    p1 = g_prob / (1.0 + e2)
    gates = jnp.where(lane == i1, p1, jnp.where(lane == i2, p1 * e2, 0.0))
    gate_ref[...] = jnp.where(lane == 0, g_sel.astype(F32), gates)


def _out_proj(mix, xa, x, w_out, ln_g, ln_b, w_route, b_route, *, tm, alpha):
    t, d = x.shape
    wm = w_out[:MIX_WIDTH].astype(BF16)
    wx = w_out[MIX_WIDTH:].astype(BF16)
    row = lambda v: v.reshape(1, -1).astype(F32)
    full = lambda a: pl.BlockSpec(a.shape, lambda i: (0, 0))
    ins = [mix, xa, x, wm, wx, row(ln_g), row(ln_b), w_route, b_route]
    in_specs = [pl.BlockSpec((tm, MIX_WIDTH), lambda i: (i, 0)),
                pl.BlockSpec((tm, XA_WIDTH), lambda i: (i, 0)),
                pl.BlockSpec((tm, d), lambda i: (i, 0))] + [full(a) for a in ins[3:]]
    return pl.pallas_call(
        functools.partial(_out_kernel, alpha=alpha), grid=(t // tm,), in_specs=in_specs,
        out_specs=[pl.BlockSpec((tm, d), lambda i: (i, 0)), pl.BlockSpec((tm, LANES), lambda i: (i, 0))],
        out_shape=[jax.ShapeDtypeStruct((t, d), F32), jax.ShapeDtypeStruct((t, LANES), F32)],
        compiler_params=_cparams("parallel"))(*ins)


MOE_CHUNK = 2048
MOE_TILE = 256
MOE_UNROLL = 8


def _moe_kernel(order_ref, meta_ref, x_ref, gate_ref, wg_ref, wu_ref, wd_ref, g_ref, b_ref, o_ref,
                xs_ref, gs_ref, ys_ref, tmp_ref, *, alpha, ch, tile):
    e = pl.program_id(1)
    grp = e // EXPERTS_PER_GROUP
    k = e % EXPERTS_PER_GROUP
    start = meta_ref[0, 0, grp]
    n_rows = meta_ref[0, 0, N_GROUPS + grp]
    n_tiles = (n_rows + (tile - 1)) // tile

    def token(pos):
        return order_ref[0, 0, jnp.minimum(start + pos, ch - 1)]

    @pl.when(k == 0)
    def _():
        def gather_tile(rt, carry):
            r0 = pl.multiple_of(rt * tile, tile)

            def rows(blk, c2):
                for u in range(MOE_UNROLL):
                    i = blk * MOE_UNROLL + u
                    t = token(r0 + i)
                    tmp_ref[pl.ds(i, 1), :] = x_ref[pl.ds(t, 1), :]
                    gs_ref[pl.ds(r0 + i, 1), :] = gate_ref[pl.ds(t, 1), :]
                return c2
            lax.fori_loop(0, tile // MOE_UNROLL, rows, 0)
            xs_ref[pl.ds(r0, tile), :] = tmp_ref[...].astype(BF16)
            return carry
        lax.fori_loop(0, n_tiles, gather_tile, 0)

    lane = lax.broadcasted_iota(jnp.int32, (tile, LANES), 1)

    def expert_tile(rt, carry):
        r0 = pl.multiple_of(rt * tile, tile)
        xb = xs_ref[pl.ds(r0, tile), :]
        gcol = jnp.sum(jnp.where(lane == ROUTE_OFF + e, gs_ref[pl.ds(r0, tile), :], 0.0), -1, keepdims=True)
        hg = _dot(xb, wg_ref[0])
        h = hg * _sigmoid(hg) * _dot(xb, wu_ref[0]) * gcol
        y = _dot(h.astype(BF16), wd_ref[0])

        @pl.when(k == 0)
        def _():
            ys_ref[pl.ds(r0, tile), :] = y

        @pl.when(k > 0)
        def _():
            ys_ref[pl.ds(r0, tile), :] += y
        return carry
    lax.fori_loop(0, n_tiles, expert_tile, 0)

    @pl.when(k == EXPERTS_PER_GROUP - 1)
    def _():
        def rows(blk, c2):
            for u in range(MOE_UNROLL):
                i = blk * MOE_UNROLL + u
                o_ref[pl.ds(token(i), 1), :] = ys_ref[pl.ds(i, 1), :]
            return c2
        lax.fori_loop(0, n_tiles * (tile // MOE_UNROLL), rows, 0)

    @pl.when(e == pl.num_programs(1) - 1)
    def _():
        for r0 in range(0, ch, tile):
            z = alpha * x_ref[r0:r0 + tile, :] + o_ref[r0:r0 + tile, :]
            o_ref[r0:r0 + tile, :] = _layer_norm(z, g_ref[...], b_ref[...])


def _moe(x1, gate, wg, wu, wd, ln_g, ln_b, *, alpha):
    t, d = x1.shape
    ne, _, ff = wg.shape
    ch = min(MOE_CHUNK, t)
    nc = t // ch
    gsel = gate[:, 0].astype(jnp.int32).reshape(nc, ch)
    order = jnp.argsort(gsel, axis=1, stable=True).astype(jnp.int32)
    counts = jnp.sum(gsel[:, :, None] == jnp.arange(N_GROUPS, dtype=jnp.int32), axis=1, dtype=jnp.int32)
    starts = jnp.cumsum(counts, axis=1, dtype=jnp.int32) - counts
    meta = jnp.concatenate([starts, counts], axis=1)
    row = lambda v: v.reshape(1, -1).astype(F32)
    smem = lambda n: pl.BlockSpec((1, 1, n), lambda c, e: (c, 0, 0), memory_space=pltpu.SMEM)
    return pl.pallas_call(
        functools.partial(_moe_kernel, alpha=alpha, ch=ch, tile=MOE_TILE), grid=(nc, ne),
        in_specs=[smem(ch), smem(2 * N_GROUPS),
                  pl.BlockSpec((ch, d), lambda c, e: (c, 0), pipeline_mode=pl.Buffered(1)),
                  pl.BlockSpec((ch, LANES), lambda c, e: (c, 0)),
                  pl.BlockSpec((1, d, ff), lambda c, e: (e, 0, 0)),
                  pl.BlockSpec((1, d, ff), lambda c, e: (e, 0, 0)),
                  pl.BlockSpec((1, ff, d), lambda c, e: (e, 0, 0)),
                  pl.BlockSpec((1, d), lambda c, e: (0, 0)),
                  pl.BlockSpec((1, d), lambda c, e: (0, 0))],
        out_specs=pl.BlockSpec((ch, d), lambda c, e: (c, 0)),
        out_shape=jax.ShapeDtypeStruct((t, d), F32),
        scratch_shapes=[pltpu.VMEM((ch, d), BF16), pltpu.VMEM((ch, LANES), F32), pltpu.VMEM((ch, d), F32),
                        pltpu.VMEM((MOE_TILE, d), F32)],
        compiler_params=_cparams("parallel", "arbitrary"))(
            order.reshape(nc, 1, ch), meta.reshape(nc, 1, 2 * N_GROUPS), x1, gate,
            wg.astype(BF16), wu.astype(BF16), wd.astype(BF16), row(ln_g), row(ln_b))


def _rope_tables(positions, dim, theta, period):
    b, s = positions.shape
    half = dim // 2
    inv = theta ** (-jnp.arange(0, dim, 2, dtype=F32) / dim)
    ang = positions.astype(F32)[..., None] * inv
    cos, sin = jnp.cos(ang), jnp.sin(ang)
    rest = period - dim
    one = jnp.ones((b, s, rest), F32)
    zero_h = jnp.zeros((b, s, half), F32)
    zero_r = jnp.zeros((b, s, rest), F32)
    c = jnp.concatenate([cos, cos, one], -1)
    s1 = jnp.concatenate([-sin, zero_h, zero_r], -1)
    s2 = jnp.concatenate([zero_h, sin, zero_r], -1)
    rep = LANES // period
    return tuple(jnp.tile(t, (1, 1, rep)) for t in (c, s1, s2))


def _heads(a, b, s, h):
    return a.reshape(b, s, h, HEAD_DIM).transpose(0, 2, 1, 3)


def _nsa_mixer(x2, b, s, w_in, cmp_pe, cmp_w1, cmp_w2, tabs_n, tabs_n3, *, tm, tq):
    kv, g, d = NSA_KV_HEADS, NSA_GROUP, HEAD_DIM
    offs = np.cumsum([0, MIX_WIDTH] + [NSA_KV_WIDTH] * 6 + [3 * MIX_HEADS, XA_WIDTH]).tolist()
    col = lambda k: w_in[:, offs[k]:offs[k + 1]]
    wq, wkc, wvc, wks, wvs, wkw, wvw, wgt, wxq = [col(k) for k in range(9)]
    w_rope = jnp.concatenate([wq, wks, wkw], 1).astype(BF16)
    w_plain = jnp.concatenate([wkc, wvc, wvs, wvw], 1).astype(BF16)
    w_gate = jnp.zeros((D_MODEL, LANES), F32).at[:, :3 * MIX_HEADS].set(wgt).astype(BF16)
    flat = lambda t: t.reshape(b * s, LANES)
    pr, pp, pg, xq = _proj(x2, [w_rope, w_plain, w_gate, wxq.astype(BF16)], [BF16, BF16, F32, BF16], tm=tm,
                           rope=(ROT_DIM // 2,) + tuple(flat(t) for t in tabs_n),
                           ropes=(True, False, False, False), scales=(1.0, 1.0, 1.0, XA_SCALE))
    q = _heads(pr[:, :MIX_WIDTH], b, s, MIX_HEADS).reshape(b * kv, g, s, d)
    ks = _heads(pr[:, MIX_WIDTH:MIX_WIDTH + NSA_KV_WIDTH], b, s, kv).reshape(b * kv, s, d)
    kw = _heads(pr[:, MIX_WIDTH + NSA_KV_WIDTH:], b, s, kv).reshape(b * kv, s, d)
    w4 = NSA_KV_WIDTH
    kc = _heads(pp[:, 0:w4], b, s, kv).reshape(b * kv, s // CMP_STRIDE, CMP_STRIDE * d)
    vc = _heads(pp[:, w4:2 * w4], b, s, kv).reshape(b * kv, s // CMP_STRIDE, CMP_STRIDE * d)
    vs = _heads(pp[:, 2 * w4:3 * w4], b, s, kv).reshape(b * kv, s, d)
    vw = _heads(pp[:, 3 * w4:4 * w4], b, s, kv).reshape(b * kv, s, d)
    gl = pg[:, :3 * MIX_HEADS].reshape(b, s, MIX_HEADS, 3).transpose(0, 2, 1, 3).reshape(b * kv, g, s, 3)

    n_cmp = (s - CMP_LEN) // CMP_STRIDE + 1
    nch = s // CMP_STRIDE
    end_tabs = tuple(t[:, CMP_LEN - 1::CMP_STRIDE][:, :nch] for t in tabs_n3)
    end_tabs = tuple(jnp.pad(t, ((0, 0), (0, nch - t.shape[1]), (0, 0))) for t in end_tabs)
    k_cmp = _compress(kc, cmp_pe[0], cmp_w1[0], cmp_w2[0], end_tabs, rope=True, kvh=kv)
    v_cmp = _compress(vc, cmp_pe[1], cmp_w1[1], cmp_w2[1], end_tabs, rope=False, kvh=kv)

    n_sel = s // SEL_BLOCK
    starts = np.arange(nch) * CMP_STRIDE
    sel_start = np.arange(n_sel) * SEL_BLOCK
    ovl = ((starts[:, None] < sel_start[None, :] + SEL_BLOCK)
           & (starts[:, None] + CMP_LEN > sel_start[None, :])
           & (np.arange(nch)[:, None] < n_cmp)).astype(np.float32)
    o_c, q_aug = _cmp_attn(q, k_cmp, v_cmp, jnp.asarray(ovl.T), tq=tq, n_cmp=n_cmp)

    onehot = (np.arange(s)[:, None] // SEL_BLOCK == np.arange(n_sel)[None, :]).astype(np.float32)
    k_aug = jnp.concatenate([ks, jnp.broadcast_to(jnp.asarray(onehot, BF16), (b * kv, s, n_sel))], -1)
    ft = min(FLASH_TQ, s)
    o_s = _flash(q_aug, k_aug, vs, mode="causal", tq=ft, tk=ft, db=FLASH_DB, out_dtype=F32)
    kw_aug = jnp.concatenate([kw, jnp.zeros((b * kv, s, n_sel), BF16)], -1)
    o_w = _flash(q_aug, kw_aug, vw, mode="window", tq=WINDOW, tk=WINDOW, out_dtype=F32)
    o = _combine(gl, o_c, o_s, o_w, tq=tq)
    mix = o.reshape(b, MIX_HEADS, s, d).transpose(0, 2, 1, 3).reshape(b * s, MIX_WIDTH)
    return mix, xq


def _mla_mixer(x2, b, s, w_in, q_norm, w_uq, kv_norm, w_ukv, tabs_m, *, tm, tq):
    h = MIX_HEADS
    w_c = w_in[:, :MLA_Q_RANK + MLA_KV_RANK].astype(BF16)
    w_kr = jnp.zeros((D_MODEL, LANES), F32).at[:, :MLA_ROPE].set(
        w_in[:, MLA_Q_RANK + MLA_KV_RANK:MLA_Q_RANK + MLA_KV_RANK + MLA_ROPE]).astype(BF16)
    w_xq = w_in[:, -XA_WIDTH:].astype(BF16)
    flat = tuple(t.reshape(b * s, LANES) for t in tabs_m)
    c, kr, xq = _proj(x2, [w_c, w_kr, w_xq], [F32, BF16, BF16], tm=tm,
                      rope=(MLA_ROPE // 2,) + flat, ropes=(False, True, False),
                      scales=(1.0, 1.0, XA_SCALE))
    k_rope = kr[:, :MLA_ROPE]
    wq3 = w_uq.reshape(MLA_Q_RANK, h, MLA_NOPE + MLA_ROPE)
    wq_nope = wq3[:, :, :MLA_NOPE].reshape(MLA_Q_RANK, h * MLA_NOPE).astype(BF16)
    wq_rope = wq3[:, :, MLA_NOPE:].reshape(MLA_Q_RANK, h * MLA_ROPE).astype(BF16)
    sc = LOG2E / math.sqrt(MLA_NOPE + MLA_ROPE)
    qn, qr = _proj(c[:, :MLA_Q_RANK], [wq_nope, wq_rope], [BF16, BF16], tm=tm, rms_gain=q_norm,
                   rope=(MLA_ROPE // 2,) + flat, ropes=(False, True), scales=(sc, sc))
    (kvu,) = _proj(c[:, MLA_Q_RANK:], [w_ukv.astype(BF16)], [BF16], tm=tm, rms_gain=kv_norm)
    kvu = kvu.reshape(b, s, h, MLA_NOPE + MLA_V)
    q = jnp.concatenate([qn.reshape(b, s, h, MLA_NOPE), qr.reshape(b, s, h, MLA_ROPE)], -1)
    k = jnp.concatenate([kvu[..., :MLA_NOPE],
                         jnp.broadcast_to(k_rope.reshape(b, s, 1, MLA_ROPE), (b, s, h, MLA_ROPE))], -1)
    q = q.transpose(0, 2, 1, 3).reshape(b * h, 1, s, MLA_NOPE + MLA_ROPE)
    k = k.transpose(0, 2, 1, 3).reshape(b * h, s, MLA_NOPE + MLA_ROPE)
    v = kvu[..., MLA_NOPE:].transpose(0, 2, 1, 3).reshape(b * h, s, MLA_V)
    ft = min(FLASH_TQ, s)
    o = _flash(q, k, v, mode="causal", tq=ft, tk=ft, db=FLASH_DB, out_dtype=BF16)
    mix = o.reshape(b, h, s, MLA_V).transpose(0, 2, 1, 3).reshape(b * s, h * MLA_V)
    return mix, xq


def _conv_mixer(x2, b, s, w_in, b_in, dw_w, dw_b, ln_g, ln_b, *, tm, ts):
    a, xq = _proj(x2, [w_in[:, :2 * CONV_CH].astype(BF16), w_in[:, 2 * CONV_CH:].astype(BF16)],
                  [F32, BF16], tm=tm, scales=(1.0, XA_SCALE))
    mix = _conv(a.reshape(b, s, 2 * CONV_CH), b_in, dw_w, dw_b, ln_g, ln_b, ts=ts)
    return mix.reshape(b * s, CONV_CH), xq


def kernel(x, mem, positions, nsa_w_in, nsa_cmp_pe, nsa_cmp_w1, nsa_cmp_w2, mla_w_in, mla_q_norm, mla_w_uq, mla_kv_norm, mla_w_ukv, conv_w_in, conv_b_in, conv_dw_w, conv_dw_b, conv_ln_g, conv_ln_b, mem_w_kv, w_out, ln_g, ln_b, moe_w_grp, moe_b_grp, moe_w_exp, moe_b_exp, moe_w_gate, moe_w_up, moe_w_down):
    b, s, d = x.shape
    depth = w_out.shape[0]
    n_mem = mem.shape[1]
    alpha = (2.0 * depth) ** 0.25
    tm = 512
    tq = 512
    tabs_n3 = _rope_tables(positions, ROT_DIM, ROPE_THETA, HEAD_DIM)
    tabs_m3 = _rope_tables(positions, MLA_ROPE, MLA_THETA, MLA_ROPE)
    x2 = x.reshape(b * s, d)
    mem2 = mem.reshape(b * n_mem, d)
    for i in range(depth):
        kind, j = i % N_MIXERS, i // N_MIXERS
        if kind == 0:
            mix, xq = _nsa_mixer(x2, b, s, nsa_w_in[j], nsa_cmp_pe[j], nsa_cmp_w1[j], nsa_cmp_w2[j],
                                 tabs_n3, tabs_n3, tm=tm, tq=tq)
        elif kind == 1:
            mix, xq = _mla_mixer(x2, b, s, mla_w_in[j], mla_q_norm[j], mla_w_uq[j], mla_kv_norm[j],
                                 mla_w_ukv[j], tabs_m3, tm=tm, tq=tq)
        else:
            mix, xq = _conv_mixer(x2, b, s, conv_w_in[j], conv_b_in[j], conv_dw_w[j], conv_dw_b[j],
                                  conv_ln_g[j], conv_ln_b[j], tm=tm, ts=tq)
        (mkv,) = _proj(mem2, [mem_w_kv[i].astype(BF16)], [BF16], tm=min(tm, b * n_mem))
        mk = mkv[:, :XA_WIDTH].reshape(b, n_mem, XA_HEADS, HEAD_DIM).transpose(0, 2, 1, 3)
        mv = mkv[:, XA_WIDTH:].reshape(b, n_mem, XA_HEADS, HEAD_DIM).transpose(0, 2, 1, 3)
        xa = _flash(_heads(xq, b, s, XA_HEADS).reshape(b * XA_HEADS, 1, s, HEAD_DIM), mk.reshape(b * XA_HEADS, n_mem, HEAD_DIM),
                    mv.reshape(b * XA_HEADS, n_mem, HEAD_DIM), mode="full", tq=min(FLASH_TQ, s), tk=n_mem, out_dtype=BF16)
        xa = xa.reshape(b, XA_HEADS, s, HEAD_DIM).transpose(0, 2, 1, 3).reshape(b * s, XA_WIDTH)
        w_route = jnp.zeros((d, LANES), F32).at[:, :N_GROUPS].set(moe_w_grp[i])
        w_route = w_route.at[:, ROUTE_OFF:ROUTE_OFF + N_EXPERTS].set(moe_w_exp[i])
        b_route = jnp.zeros((1, LANES), F32).at[0, :N_GROUPS].set(moe_b_grp[i])
        b_route = b_route.at[0, ROUTE_OFF:ROUTE_OFF + N_EXPERTS].set(moe_b_exp[i])
        x1, gate = _out_proj(mix, xa, x2, w_out[i], ln_g[i, 0], ln_b[i, 0], w_route, b_route,
                             tm=tm, alpha=alpha)
        x2 = _moe(x1, gate, moe_w_gate[i], moe_w_up[i], moe_w_down[i], ln_g[i, 1], ln_b[i, 1], alpha=alpha)
    return x2.reshape(b, s, d)
```

```python
import functools
import math

import numpy as np
import jax
import jax.numpy as jnp
from jax import lax
from jax.experimental import pallas as pl
from jax.experimental.pallas import tpu as pltpu

F32 = jnp.float32
BF16 = jnp.bfloat16

D_MODEL = 1024
N_MIXERS = 3
HEAD_DIM = 64
MIX_HEADS = 12
MIX_WIDTH = MIX_HEADS * HEAD_DIM
XA_HEADS = 4
XA_WIDTH = XA_HEADS * HEAD_DIM
ROPE_THETA = 500000.0
ROT_DIM = HEAD_DIM // 4
LN_EPS = 1e-5
RMS_EPS = 1e-6
NSA_KV_HEADS = 4
NSA_GROUP = MIX_HEADS // NSA_KV_HEADS
NSA_KV_WIDTH = NSA_KV_HEADS * HEAD_DIM
CMP_LEN = 32
CMP_STRIDE = 16
CMP_HIDDEN = 128
SEL_BLOCK = 64
SEL_SHIFT = 6
SEL_TOPK = 16
WINDOW = 512
MLA_Q_RANK = 256
MLA_KV_RANK = 128
MLA_NOPE = 64
MLA_ROPE = 32
MLA_V = 64
MLA_THETA = 10000.0
CONV_CH = MIX_WIDTH
CONV_WIDTH = 31
N_GROUPS = 4
EXPERTS_PER_GROUP = 4
N_EXPERTS = N_GROUPS * EXPERTS_PER_GROUP
EXPERT_FF = 512

LANES = 128
SUBLANES = 8
MXU_N = 256
VMEM_LIMIT = 56 * 1024 * 1024
NEG = -1e30
ROUTE_OFF = N_GROUPS
LOG2E = math.log2(math.e)
XA_SCALE = LOG2E / math.sqrt(HEAD_DIM)
FLASH_TQ = 1024
FLASH_DB = 512


def _cparams(*sem):
    return pltpu.CompilerParams(dimension_semantics=sem, vmem_limit_bytes=VMEM_LIMIT)


def _dot(a, b):
    return jnp.dot(a, b, preferred_element_type=F32)


def _dot_nt(a, b):
    return lax.dot_general(a, b, (((1,), (1,)), ((), ())), preferred_element_type=F32)


def _layer_norm(z, g, b):
    mu = jnp.mean(z, -1, keepdims=True)
    zc = z - mu
    var = jnp.mean(zc * zc, -1, keepdims=True)
    return zc * lax.rsqrt(var + LN_EPS) * g + b


def _sigmoid(x):
    return 1.0 / (1.0 + jnp.exp(-x))


def _rope(acc, r, c_ref, s1_ref, s2_ref):
    n = acc.shape[1]
    rep = n // LANES
    c = jnp.tile(c_ref[...], (1, rep))
    s1 = jnp.tile(s1_ref[...], (1, rep))
    s2 = jnp.tile(s2_ref[...], (1, rep))
    return acc * c + pltpu.roll(acc, n - r, 1) * s1 + pltpu.roll(acc, r, 1) * s2


class _Out:
    def __init__(self, w, dtype, rope=False, scale=1.0, heads=None, pad=None):
        self.w, self.dtype, self.rope, self.scale, self.heads, self.pad = w, dtype, rope, scale, heads, pad


def _proj_kernel(*refs, outs, rms, rope_r, has_extra, tm, spt):
    it = iter(refs)
    x_ref = next(it)
    g_ref = next(it) if rms else None
    tabs = (next(it), next(it), next(it)) if rope_r else None
    extra_ref = next(it) if has_extra else None
    w_refs = [next(it) for _ in outs]
    o_refs = [next(it) for _ in outs]
    xv = x_ref[...]
    if rms:
        xf = xv.astype(F32)
        xf = xf * lax.rsqrt(jnp.mean(xf * xf, -1, keepdims=True) + RMS_EPS) * g_ref[...]
        xb = xf.astype(BF16)
    else:
        xb = xv.astype(BF16)

    def pad_tile(kind, width):
        if kind == "onehot":
            tok = (pl.program_id(0) % spt) * tm + lax.broadcasted_iota(jnp.int32, (tm, width), 0)
            blk = lax.broadcasted_iota(jnp.int32, (tm, width), 1)
            return jnp.where(jnp.right_shift(tok, SEL_SHIFT) == blk, 1.0, 0.0)
        if kind == "extra":
            return extra_ref[:, :width].astype(F32)
        return jnp.full((tm, width), 1.0 if kind == "ones" else 0.0, F32)

    for o, w_ref, o_ref in zip(outs, w_refs, o_refs):
        n = w_ref.shape[1]
        step = MXU_N if n % MXU_N == 0 else LANES
        pad = pad_tile(*o.pad) if o.pad else None
        for c0 in range(0, n, step):
            acc = _dot(xb, w_ref[:, c0:c0 + step])
            if o.rope:
                acc = _rope(acc, rope_r, *tabs)
            if o.scale != 1.0:
                acc = acc * o.scale
            if o.heads is None:
                o_ref[:, c0:c0 + step] = acc.astype(o_ref.dtype)
            else:
                hw = o.heads[1]
                for j in range(step // hw):
                    piece = acc[:, j * hw:(j + 1) * hw]
                    if pad is not None:
                        piece = jnp.concatenate([piece, pad], axis=1)
                    o_ref[0, c0 // hw + j] = piece.astype(o_ref.dtype)


def _proj(x, outs, *, tm, seq, x_cols=None, rms_gain=None, rope=None, extra=None):
    m = x.shape[0]
    tm = min(tm, seq)
    spt = seq // tm
    kdim, xj = x_cols if x_cols else (x.shape[1], 0)
    ins = [x]
    in_specs = [pl.BlockSpec((tm, kdim), lambda i: (i, xj))]
    if rms_gain is not None:
        ins.append(rms_gain.reshape(1, kdim).astype(F32))
        in_specs.append(pl.BlockSpec((1, kdim), lambda i: (0, 0)))
    rope_r = 0
    if rope is not None:
        rope_r = rope[0]
        for t in rope[1:]:
            ins.append(t)
            in_specs.append(pl.BlockSpec((tm, LANES), lambda i: (i, 0)))
    if extra is not None:
        ins.append(extra)
        in_specs.append(pl.BlockSpec((tm, LANES), lambda i: (i, 0)))
    for o in outs:
        ins.append(o.w)
        in_specs.append(pl.BlockSpec(o.w.shape, lambda i: (0, 0)))
    out_shape, out_specs = [], []
    for o in outs:
        if o.heads is None:
            out_shape.append(jax.ShapeDtypeStruct((m, o.w.shape[1]), o.dtype))
            out_specs.append(pl.BlockSpec((tm, o.w.shape[1]), lambda i: (i, 0)))
        else:
            nh, hw = o.heads
            assert nh * hw == o.w.shape[1]
            width = hw + (o.pad[1] if o.pad else 0)
            out_shape.append(jax.ShapeDtypeStruct((m // seq, nh, seq, width), o.dtype))
            out_specs.append(pl.BlockSpec((1, nh, tm, width), lambda i: (i // spt, 0, i % spt, 0)))
    kern = functools.partial(_proj_kernel, outs=tuple(outs), rms=rms_gain is not None, rope_r=rope_r,
                             has_extra=extra is not None, tm=tm, spt=spt)
    return pl.pallas_call(kern, grid=(m // tm,), in_specs=in_specs, out_specs=out_specs,
                          out_shape=out_shape, compiler_params=_cparams("parallel"))(*ins)


def _flash_kernel(q_ref, k_ref, v_ref, o_ref, m_ref, acc_ref, *, mode, g, tq, tk, db):
    i = pl.program_id(1)
    dv = o_ref.shape[-1]
    m_ref[...] = jnp.full(m_ref.shape, NEG, F32)
    acc_ref[...] = jnp.zeros(acc_ref.shape, F32)

    def block(h, r0, nr, kstart, nk, kind):
        kt = k_ref[0, pl.ds(kstart, nk), :]
        vt = v_ref[0, pl.ds(kstart, nk), :]
        s = _dot_nt(q_ref[0, h, r0:r0 + nr, :], kt)
        if kind is not None:
            row = lax.broadcasted_iota(jnp.int32, (nr, nk), 0)
            col = lax.broadcasted_iota(jnp.int32, (nr, nk), 1)
            s = jnp.where(col <= row if kind == "diag" else col > row, s, NEG)
        a0 = h * tq + r0
        m_prev = m_ref[a0:a0 + nr, :]
        m_new = jnp.maximum(m_prev, jnp.max(s, -1, keepdims=True))
        alpha = jnp.exp2(m_prev - m_new)
        p = jnp.exp2(s - jnp.tile(m_new, (1, nk // LANES)))
        acc_ref[a0:a0 + nr, :] = alpha * acc_ref[a0:a0 + nr, :] + _dot(p.astype(BF16), vt)
        m_ref[a0:a0 + nr, :] = m_new

    if mode == "full":
        for j in range(k_ref.shape[1] // tk):
            for h in range(g):
                block(h, 0, tq, j * tk, tk, None)
    elif mode == "window":
        t0 = pl.multiple_of(i * tq, tq)
        for h in range(g):
            block(h, 0, tq, t0, tq, "diag")

        @pl.when(i > 0)
        def _():
            for h in range(g):
                block(h, 0, tq, pl.multiple_of(t0 - tq, tq), tq, "prev")
    else:
        t0 = pl.multiple_of(i * tq, tq)
        for a in range(tq // db):
            for h in range(g):
                block(h, a * db, db, pl.multiple_of(t0 + a * db, db), db, "diag")
                for c in range(a):
                    block(h, a * db, db, pl.multiple_of(t0 + c * db, db), db, None)

        def body(j, carry):
            for h in range(g):
                block(h, 0, tq, pl.multiple_of(j * tk, tk), tk, None)
            return carry
        lax.fori_loop(0, i * (tq // tk), body, 0)

    acc = acc_ref[...]
    o = acc / pltpu.roll(acc, dv, 1)
    o_ref[0] = o[:, :dv].reshape(g, tq, dv).astype(o_ref.dtype)


def _flash(q, k, v_ones, *, mode, tq, tk, out_dtype, db=None):
    n, g, s, dk = q.shape
    sk, dv = v_ones.shape[1], v_ones.shape[2] // 2
    assert 2 * dv == LANES
    kern = functools.partial(_flash_kernel, mode=mode, g=g, tq=tq, tk=tk, db=db or tq)
    return pl.pallas_call(
        kern, grid=(n, s // tq),
        in_specs=[pl.BlockSpec((1, g, tq, dk), lambda b, i: (b, 0, i, 0)),
                  pl.BlockSpec((1, sk, dk), lambda b, i: (b, 0, 0)),
                  pl.BlockSpec((1, sk, 2 * dv), lambda b, i: (b, 0, 0))],
        out_specs=pl.BlockSpec((1, g, tq, dv), lambda b, i: (b, 0, i, 0)),
        out_shape=jax.ShapeDtypeStruct((n, g, s, dv), out_dtype),
        scratch_shapes=[pltpu.VMEM((g * tq, LANES), F32), pltpu.VMEM((g * tq, 2 * dv), F32)],
        compiler_params=_cparams("parallel", "arbitrary"))(q, k, v_ones)


def _cmp_kernel(x_ref, w1s_ref, pe_ref, w1_ref, w2_ref, c_ref, s1_ref, s2_ref, o_ref, *, rope):
    nch = x_ref.shape[1]
    ab = _dot(x_ref[0], w1s_ref[...])
    pe_term = _dot(pe_ref[...], w1_ref[...])[0:1, :]
    nxt = pltpu.roll(ab[:, CMP_HIDDEN:], nch - 1, 0)
    h = ab[:, :CMP_HIDDEN] + nxt + pe_term
    h = h * _sigmoid(h)
    o = _dot(h.astype(BF16), w2_ref[...])
    if rope:
        o = _rope(o, ROT_DIM // 2, c_ref.at[0], s1_ref.at[0], s2_ref.at[0])
    o_ref[0] = o[:, :HEAD_DIM].astype(o_ref.dtype)


def _compress(tc, pe, w1, w2, tabs, *, rope, kvh):
    n, nch, width = tc.shape
    half = CMP_STRIDE * HEAD_DIM
    w1b = w1.astype(BF16)
    w1s = jnp.concatenate([w1b[:half], w1b[half:]], axis=1)
    pe8 = jnp.zeros((8, CMP_LEN * HEAD_DIM), BF16).at[0].set(pe.reshape(-1).astype(BF16))
    w2p = jnp.zeros((CMP_HIDDEN, LANES), BF16).at[:, :HEAD_DIM].set(w2.astype(BF16))
    kern = functools.partial(_cmp_kernel, rope=rope)
    tab_spec = pl.BlockSpec((1, nch, LANES), lambda i: (i // kvh, 0, 0))
    return pl.pallas_call(
        kern, grid=(n,),
        in_specs=[pl.BlockSpec((1, nch, width), lambda i: (i, 0, 0)),
                  pl.BlockSpec(w1s.shape, lambda i: (0, 0)),
                  pl.BlockSpec(pe8.shape, lambda i: (0, 0)),
                  pl.BlockSpec(w1b.shape, lambda i: (0, 0)),
                  pl.BlockSpec(w2p.shape, lambda i: (0, 0)),
                  tab_spec, tab_spec, tab_spec],
        out_specs=pl.BlockSpec((1, nch, HEAD_DIM), lambda i: (i, 0, 0)),
        out_shape=jax.ShapeDtypeStruct((n, nch, HEAD_DIM), BF16),
        compiler_params=_cparams("parallel"))(tc, w1s, pe8, w1b, w2p, *tabs)


def _cmp_attn_kernel(q_ref, kc_ref, vc_ref, ovl_ref, oc_ref, qa_ref, *, g, tq, n_cmp, scale):
    i = pl.program_id(1)
    nch = kc_ref.shape[1]
    n_sel = ovl_ref.shape[0]
    kc = kc_ref[0]
    vc = vc_ref[0]
    t0 = i * tq
    t_row = t0 + lax.broadcasted_iota(jnp.int32, (tq, nch), 0)
    n_col = lax.broadcasted_iota(jnp.int32, (tq, nch), 1)
    vis = (n_col < n_cmp) & (n_col * CMP_STRIDE + (CMP_LEN - 1) <= t_row)
    t_lane = t0 + lax.broadcasted_iota(jnp.int32, (nch, tq), 1)
    n_sub = lax.broadcasted_iota(jnp.int32, (nch, tq), 0)
    vis_t = (n_sub < n_cmp) & (n_sub * CMP_STRIDE + (CMP_LEN - 1) <= t_lane)
    p_sum_t = jnp.zeros((nch, tq), F32)
    for h in range(g):
        qh = q_ref[0, h]
        s = jnp.where(vis, _dot_nt(qh, kc) * scale, NEG)
        m = jnp.max(s, -1, keepdims=True)
        e = jnp.where(vis, jnp.exp(s - m), 0.0)
        p = e / jnp.maximum(jnp.sum(e, -1, keepdims=True), 1e-30)
        oc_ref[0, h] = _dot(p.astype(BF16), vc).astype(oc_ref.dtype)
        st = jnp.where(vis_t, _dot_nt(kc, qh) * scale, NEG)
        mt = jnp.max(st, 0, keepdims=True)
        et = jnp.where(vis_t, jnp.exp(st - mt), 0.0)
        p_sum_t = p_sum_t + et / jnp.maximum(jnp.sum(et, 0, keepdims=True), 1e-30)
    imp = jnp.dot(ovl_ref[...], p_sum_t, preferred_element_type=F32,
                  precision=lax.Precision.HIGHEST)
    t1 = t0 + lax.broadcasted_iota(jnp.int32, (n_sel, tq), 1)
    blk = lax.broadcasted_iota(jnp.int32, (n_sel, tq), 0)
    cur = jnp.right_shift(t1, SEL_SHIFT)
    valid = blk <= cur
    forced = (blk == 0) | (blk == cur) | (blk == cur - 1)
    imp = jnp.where(valid, jnp.where(forced, jnp.inf, imp), -jnp.inf)
    nv = n_sel // SUBLANES
    groups = [imp[v * SUBLANES:(v + 1) * SUBLANES] for v in range(nv)]
    sub = lax.broadcasted_iota(jnp.int32, (SUBLANES, tq), 0)

    def count_group(rb, ranks):
        ranks = list(ranks)
        for rr in range(SUBLANES):
            row = groups[rb][rr:rr + 1, :]
            for v in range(nv):
                if v > rb:
                    ahead = jnp.where(row >= groups[v], 1.0, 0.0)
                elif v < rb:
                    ahead = jnp.where(row > groups[v], 1.0, 0.0)
                else:
                    ahead = jnp.where(sub > rr, jnp.where(row >= groups[v], 1.0, 0.0),
                                      jnp.where(row > groups[v], 1.0, 0.0))
                ranks[v] = ranks[v] + ahead
        return tuple(ranks)

    last_group = jnp.right_shift(t0 + (tq - 1), SEL_SHIFT) // SUBLANES
    ranks = tuple(jnp.zeros((SUBLANES, tq), F32) for _ in range(nv))
    for rb in range(nv):
        ranks = lax.cond(rb <= last_group, functools.partial(count_group, rb), lambda rk: rk, ranks)
    rank = jnp.concatenate(ranks, axis=0)
    chosen_t = jnp.where((rank < float(min(SEL_TOPK, n_sel))) & valid, 1.0, 0.0).astype(BF16)
    eye = (lax.broadcasted_iota(jnp.int32, (tq, tq), 0)
           == lax.broadcasted_iota(jnp.int32, (tq, tq), 1))
    chosen = _dot_nt(jnp.where(eye, 1.0, 0.0).astype(BF16), chosen_t)
    bias = (chosen - 1.0) * (-NEG)
    for h in range(g):
        qs = q_ref[0, h].astype(F32) * (scale * LOG2E)
        qa_ref[0, h] = jnp.concatenate([qs, bias], axis=1).astype(qa_ref.dtype)


def _cmp_attn(q, kc, vc, ovl_t, *, tq, n_cmp):
    n, g, s, d = q.shape
    nch = kc.shape[1]
    n_sel = ovl_t.shape[0]
    kern = functools.partial(_cmp_attn_kernel, g=g, tq=tq, n_cmp=n_cmp, scale=1.0 / math.sqrt(HEAD_DIM))
    return pl.pallas_call(
        kern, grid=(n, s // tq),
        in_specs=[pl.BlockSpec((1, g, tq, d), lambda b, i: (b, 0, i, 0)),
                  pl.BlockSpec((1, nch, d), lambda b, i: (b, 0, 0)),
                  pl.BlockSpec((1, nch, d), lambda b, i: (b, 0, 0)),
                  pl.BlockSpec(ovl_t.shape, lambda b, i: (0, 0))],
        out_specs=[pl.BlockSpec((1, g, tq, d), lambda b, i: (b, 0, i, 0)),
                   pl.BlockSpec((1, g, tq, d + n_sel), lambda b, i: (b, 0, i, 0))],
        out_shape=[jax.ShapeDtypeStruct((n, g, s, d), F32),
                   jax.ShapeDtypeStruct((n, g, s, d + n_sel), BF16)],
        compiler_params=_cparams("parallel", "arbitrary"))(q, kc, vc, ovl_t)


def _combine_kernel(gl_ref, oc_ref, os_ref, ow_ref, o_ref, *, nh):
    gates = _sigmoid(gl_ref[...])
    for h in range(nh):
        o = (gates[:, 3 * h:3 * h + 1] * oc_ref[0, h] + gates[:, 3 * h + 1:3 * h + 2] * os_ref[0, h]
             + gates[:, 3 * h + 2:3 * h + 3] * ow_ref[0, h])
        o_ref[0, h] = o.astype(o_ref.dtype)


def _combine(gl, oc, osel, ow, *, tq):
    bsz, nh, s, d = oc.shape
    spt = s // tq
    spec = pl.BlockSpec((1, nh, tq, d), lambda b, i: (b, 0, i, 0))
    return pl.pallas_call(
        functools.partial(_combine_kernel, nh=nh), grid=(bsz, spt),
        in_specs=[pl.BlockSpec((tq, LANES), lambda b, i: (b * spt + i, 0)), spec, spec, spec],
        out_specs=spec, out_shape=jax.ShapeDtypeStruct((bsz, nh, s, d), BF16),
        compiler_params=_cparams("parallel", "parallel"))(gl, oc, osel, ow)


CONV_HALO = 32
CONV_ROWS = 64


def _conv_kernel(cur_ref, prev_ref, bin_ref, w_ref, wb_ref, g_ref, b_ref, o_ref, u_ref, *, ts):
    i = pl.program_id(1)
    ch = o_ref.shape[-1]

    def glu(a):
        a = a + bin_ref[...]
        return a[:, :ch] * _sigmoid(a[:, ch:])

    u_ref[CONV_HALO:CONV_HALO + ts, :] = glu(cur_ref[0])

    @pl.when(i == 0)
    def _():
        u_ref[0:CONV_HALO, :] = jnp.zeros((CONV_HALO, ch), F32)

    @pl.when(i > 0)
    def _():
        u_ref[0:CONV_HALO, :] = glu(prev_ref[0])

    base = CONV_HALO - (CONV_WIDTH - 1)
    for r0 in range(0, ts, CONV_ROWS):
        acc = jnp.zeros((CONV_ROWS, ch), F32)
        for k in range(CONV_WIDTH):
            acc = acc + u_ref[base + r0 + k:base + r0 + k + CONV_ROWS, :] * w_ref[k:k + 1, :]
        y = _layer_norm(acc + wb_ref[...], g_ref[...], b_ref[...])
        o_ref[0, r0:r0 + CONV_ROWS, :] = (y * _sigmoid(y)).astype(o_ref.dtype)


def _conv(a, b_in, dw_w, dw_b, ln_g, ln_b, *, ts):
    bsz, s, two_ch = a.shape
    ch = two_ch // 2
    hb = ts // CONV_HALO
    row = lambda v: v.reshape(1, -1).astype(F32)
    return pl.pallas_call(
        functools.partial(_conv_kernel, ts=ts), grid=(bsz, s // ts),
        in_specs=[pl.BlockSpec((1, ts, two_ch), lambda b, i: (b, i, 0)),
                  pl.BlockSpec((1, CONV_HALO, two_ch), lambda b, i: (b, jnp.maximum(i * hb - 1, 0), 0)),
                  pl.BlockSpec((1, two_ch), lambda b, i: (0, 0)),
                  pl.BlockSpec((CONV_WIDTH, ch), lambda b, i: (0, 0)),
                  pl.BlockSpec((1, ch), lambda b, i: (0, 0)),
                  pl.BlockSpec((1, ch), lambda b, i: (0, 0)),
                  pl.BlockSpec((1, ch), lambda b, i: (0, 0))],
        out_specs=pl.BlockSpec((1, ts, ch), lambda b, i: (b, i, 0)),
        out_shape=jax.ShapeDtypeStruct((bsz, s, ch), BF16),
        scratch_shapes=[pltpu.VMEM((CONV_HALO + ts, ch), F32)],
        compiler_params=_cparams("parallel", "arbitrary"))(
            a, a, row(b_in), dw_w.astype(F32), row(dw_b), row(ln_g), row(ln_b))


def _out_kernel(mix_ref, xa_ref, x_ref, w_ref, g_ref, b_ref, wr_ref, br_ref, x1_ref, gate_ref,
                cat_ref, *, alpha, mix_heads):
    if mix_heads:
        for h in range(MIX_HEADS):
            cat_ref[:, h * HEAD_DIM:(h + 1) * HEAD_DIM] = mix_ref[0, h].astype(F32)
    else:
        cat_ref[:, :MIX_WIDTH] = mix_ref[...].astype(F32)
    for h in range(XA_HEADS):
        cat_ref[:, MIX_WIDTH + h * HEAD_DIM:MIX_WIDTH + (h + 1) * HEAD_DIM] = xa_ref[0, h].astype(F32)
    y = _dot(cat_ref[...].astype(BF16), w_ref[...])
    x1 = _layer_norm(alpha * x_ref[...] + y, g_ref[...], b_ref[...])
    x1_ref[...] = x1
    hi = x1.astype(BF16)
    lo = (x1 - hi.astype(F32)).astype(BF16)
    wr = wr_ref[...]
    w_hi = wr.astype(BF16)
    w_lo = (wr - w_hi.astype(F32)).astype(BF16)
    logits = ((_dot(hi, w_hi) + _dot(lo, w_hi)) + (_dot(hi, w_lo) + _dot(lo, w_lo)) + br_ref[...])
    lane = lax.broadcasted_iota(jnp.int32, logits.shape, 1)
    far = jnp.int32(LANES)
    lg = jnp.where(lane < N_GROUPS, logits, -jnp.inf)
    mg = jnp.max(lg, -1, keepdims=True)
    g_prob = 1.0 / jnp.sum(jnp.exp(lg - mg), -1, keepdims=True)
    g_sel = jnp.min(jnp.where(lg == mg, lane, far), -1, keepdims=True)
    lo = ROUTE_OFF + EXPERTS_PER_GROUP * g_sel
    le = jnp.where((lane >= lo) & (lane < lo + EXPERTS_PER_GROUP), logits, -jnp.inf)
    m1 = jnp.max(le, -1, keepdims=True)
    i1 = jnp.min(jnp.where(le == m1, lane, far), -1, keepdims=True)
    le2 = jnp.where(lane == i1, -jnp.inf, le)
    m2 = jnp.max(le2, -1, keepdims=True)
    i2 = jnp.min(jnp.where(le2 == m2, lane, far), -1, keepdims=True)
    e2 = jnp.exp(m2 - m1)
    p1 = g_prob / (1.0 + e2)
    gates = jnp.where(lane == i1, p1, jnp.where(lane == i2, p1 * e2, 0.0))
    gate_ref[...] = jnp.where(lane == 0, g_sel.astype(F32), gates)


def _out_proj(mix, xa, x, w_out, ln_g, ln_b, w_route, b_route, *, tm, seq, alpha):
    t, d = x.shape
    tm = min(tm, seq)
    spt = seq // tm
    mix_heads = mix.ndim == 4
    row = lambda v: v.reshape(1, -1).astype(F32)
    full = lambda a: pl.BlockSpec(a.shape, lambda i: (0, 0))
    heads = lambda nh: pl.BlockSpec((1, nh, tm, HEAD_DIM), lambda i: (i // spt, 0, i % spt, 0))
    ins = [mix, xa, x, w_out.astype(BF16), row(ln_g), row(ln_b), w_route, b_route]
    in_specs = [heads(MIX_HEADS) if mix_heads else pl.BlockSpec((tm, MIX_WIDTH), lambda i: (i, 0)),
                heads(XA_HEADS),
                pl.BlockSpec((tm, d), lambda i: (i, 0))] + [full(a) for a in ins[3:]]
    return pl.pallas_call(
        functools.partial(_out_kernel, alpha=alpha, mix_heads=mix_heads), grid=(t // tm,), in_specs=in_specs,
        out_specs=[pl.BlockSpec((tm, d), lambda i: (i, 0)), pl.BlockSpec((tm, LANES), lambda i: (i, 0))],
        out_shape=[jax.ShapeDtypeStruct((t, d), F32), jax.ShapeDtypeStruct((t, LANES), F32)],
        scratch_shapes=[pltpu.VMEM((tm, MIX_WIDTH + XA_WIDTH), F32)],
        compiler_params=_cparams("parallel"))(*ins)


MOE_CHUNK = 2048
MOE_TILE = 256
MOE_UNROLL = 8


def _moe_kernel(order_ref, meta_ref, x_ref, gate_ref, wg_ref, wu_ref, wd_ref, g_ref, b_ref, o_ref,
                xs_ref, gs_ref, ys_ref, tmp_ref, *, alpha, ch, tile):
    e = pl.program_id(1)
    grp = e // EXPERTS_PER_GROUP
    k = e % EXPERTS_PER_GROUP
    start = meta_ref[0, 0, grp]
    n_rows = meta_ref[0, 0, N_GROUPS + grp]
    n_tiles = (n_rows + (tile - 1)) // tile

    def token(pos):
        return order_ref[0, 0, jnp.minimum(start + pos, ch - 1)]

    @pl.when(k == 0)
    def _():
        def gather_tile(rt, carry):
            r0 = pl.multiple_of(rt * tile, tile)

            def rows(blk, c2):
                for u in range(MOE_UNROLL):
                    i = blk * MOE_UNROLL + u
                    t = token(r0 + i)
                    tmp_ref[pl.ds(i, 1), :] = x_ref[pl.ds(t, 1), :]
                    gs_ref[pl.ds(r0 + i, 1), :] = gate_ref[pl.ds(t, 1), :]
                return c2
            lax.fori_loop(0, tile // MOE_UNROLL, rows, 0)
            xs_ref[pl.ds(r0, tile), :] = tmp_ref[...].astype(BF16)
            return carry
        lax.fori_loop(0, n_tiles, gather_tile, 0)

    lane = lax.broadcasted_iota(jnp.int32, (tile, LANES), 1)

    def expert_tile(rt, carry):
        r0 = pl.multiple_of(rt * tile, tile)
        xb = xs_ref[pl.ds(r0, tile), :]
        gcol = jnp.sum(jnp.where(lane == ROUTE_OFF + e, gs_ref[pl.ds(r0, tile), :], 0.0), -1, keepdims=True)
        hg = _dot(xb, wg_ref[0])
        h = hg * _sigmoid(hg) * _dot(xb, wu_ref[0]) * gcol
        y = _dot(h.astype(BF16), wd_ref[0])

        @pl.when(k == 0)
        def _():
            ys_ref[pl.ds(r0, tile), :] = y

        @pl.when(k > 0)
        def _():
            ys_ref[pl.ds(r0, tile), :] += y
        return carry
    lax.fori_loop(0, n_tiles, expert_tile, 0)

    @pl.when(k == EXPERTS_PER_GROUP - 1)
    def _():
        def rows(blk, c2):
            for u in range(MOE_UNROLL):
                i = blk * MOE_UNROLL + u
                o_ref[pl.ds(token(i), 1), :] = ys_ref[pl.ds(i, 1), :]
            return c2
        lax.fori_loop(0, n_tiles * (tile // MOE_UNROLL), rows, 0)

    @pl.when(e == pl.num_programs(1) - 1)
    def _():
        for r0 in range(0, ch, tile):
            z = alpha * x_ref[r0:r0 + tile, :] + o_ref[r0:r0 + tile, :]
            o_ref[r0:r0 + tile, :] = _layer_norm(z, g_ref[...], b_ref[...])


def _moe(x1, gate, wg, wu, wd, ln_g, ln_b, *, alpha):
    t, d = x1.shape
    ne, _, ff = wg.shape
    ch = min(MOE_CHUNK, t)
    nc = t // ch
    gsel = gate[:, 0].astype(jnp.int32).reshape(nc, ch)
    order = jnp.argsort(gsel, axis=1, stable=True).astype(jnp.int32)
    counts = jnp.sum(gsel[:, :, None] == jnp.arange(N_GROUPS, dtype=jnp.int32), axis=1, dtype=jnp.int32)
    starts = jnp.cumsum(counts, axis=1, dtype=jnp.int32) - counts
    meta = jnp.concatenate([starts, counts], axis=1)
    row = lambda v: v.reshape(1, -1).astype(F32)
    smem = lambda n: pl.BlockSpec((1, 1, n), lambda c, e: (c, 0, 0), memory_space=pltpu.SMEM)
    return pl.pallas_call(
        functools.partial(_moe_kernel, alpha=alpha, ch=ch, tile=MOE_TILE), grid=(nc, ne),
        in_specs=[smem(ch), smem(2 * N_GROUPS),
                  pl.BlockSpec((ch, d), lambda c, e: (c, 0), pipeline_mode=pl.Buffered(1)),
                  pl.BlockSpec((ch, LANES), lambda c, e: (c, 0)),
                  pl.BlockSpec((1, d, ff), lambda c, e: (e, 0, 0)),
                  pl.BlockSpec((1, d, ff), lambda c, e: (e, 0, 0)),
                  pl.BlockSpec((1, ff, d), lambda c, e: (e, 0, 0)),
                  pl.BlockSpec((1, d), lambda c, e: (0, 0)),
                  pl.BlockSpec((1, d), lambda c, e: (0, 0))],
        out_specs=pl.BlockSpec((ch, d), lambda c, e: (c, 0)),
        out_shape=jax.ShapeDtypeStruct((t, d), F32),
        scratch_shapes=[pltpu.VMEM((ch, d), BF16), pltpu.VMEM((ch, LANES), F32), pltpu.VMEM((ch, d), F32),
                        pltpu.VMEM((MOE_TILE, d), F32)],
        compiler_params=_cparams("parallel", "arbitrary"))(
            order.reshape(nc, 1, ch), meta.reshape(nc, 1, 2 * N_GROUPS), x1, gate,
            wg.astype(BF16), wu.astype(BF16), wd.astype(BF16), row(ln_g), row(ln_b))


def _rope_tables(positions, dim, theta, period, offset=0):
    b, s = positions.shape
    half = dim // 2
    inv = theta ** (-jnp.arange(0, dim, 2, dtype=F32) / dim)
    ang = positions.astype(F32)[..., None] * inv
    cos, sin = jnp.cos(ang), jnp.sin(ang)
    rest = period - dim - offset
    fill = lambda v, n: jnp.full((b, s, n), v, F32)
    c = jnp.concatenate([fill(1.0, offset), cos, cos, fill(1.0, rest)], -1)
    s1 = jnp.concatenate([fill(0.0, offset), -sin, fill(0.0, half + rest)], -1)
    s2 = jnp.concatenate([fill(0.0, offset + half), sin, fill(0.0, rest)], -1)
    rep = LANES // period
    return tuple(jnp.tile(t, (1, 1, rep)).reshape(b * s, LANES) for t in (c, s1, s2))


def _nsa_mixer(x2, b, s, w_in, cmp_pe, cmp_w1, cmp_w2, tabs_n, *, tm, tq):
    kv, g, d = NSA_KV_HEADS, NSA_GROUP, HEAD_DIM
    offs = np.cumsum([0, MIX_WIDTH] + [NSA_KV_WIDTH] * 6 + [3 * MIX_HEADS, XA_WIDTH]).tolist()
    col = lambda k: w_in[:, offs[k]:offs[k + 1]]
    wq, wkc, wvc, wks, wvs, wkw, wvw, wgt, wxq = [col(k) for k in range(9)]
    w_gate = jnp.zeros((D_MODEL, LANES), F32).at[:, :3 * MIX_HEADS].set(wgt).astype(BF16)
    n_sel = s // SEL_BLOCK
    bf = lambda w: w.astype(BF16)
    kvh = (kv, d)
    q, k_aug, kw_aug, kc, vc, vs1, vw1, pg, xq = _proj(
        x2, [_Out(bf(wq), BF16, rope=True, heads=(MIX_HEADS, d)),
             _Out(bf(wks), BF16, rope=True, heads=kvh, pad=("onehot", n_sel)),
             _Out(bf(wkw), BF16, rope=True, heads=kvh, pad=("zeros", n_sel)),
             _Out(bf(wkc), BF16, heads=kvh), _Out(bf(wvc), BF16, heads=kvh),
             _Out(bf(wvs), BF16, heads=kvh, pad=("ones", d)), _Out(bf(wvw), BF16, heads=kvh, pad=("ones", d)),
             _Out(w_gate, F32), _Out(bf(wxq), BF16, scale=XA_SCALE, heads=(XA_HEADS, d))],
        tm=tm, seq=s, rope=(ROT_DIM // 2,) + tuple(tabs_n))
    q = q.reshape(b * kv, g, s, d)
    nch = s // CMP_STRIDE
    kc = kc.reshape(b * kv, nch, CMP_STRIDE * d)
    vc = vc.reshape(b * kv, nch, CMP_STRIDE * d)

    n_cmp = (s - CMP_LEN) // CMP_STRIDE + 1
    end_tabs = tuple(t.reshape(b, s, LANES)[:, CMP_LEN - 1::CMP_STRIDE][:, :nch] for t in tabs_n)
    end_tabs = tuple(jnp.pad(t, ((0, 0), (0, nch - t.shape[1]), (0, 0))) for t in end_tabs)
    k_cmp = _compress(kc, cmp_pe[0], cmp_w1[0], cmp_w2[0], end_tabs, rope=True, kvh=kv)
    v_cmp = _compress(vc, cmp_pe[1], cmp_w1[1], cmp_w2[1], end_tabs, rope=False, kvh=kv)

    starts = np.arange(nch) * CMP_STRIDE
    sel_start = np.arange(n_sel) * SEL_BLOCK
    ovl = ((starts[:, None] < sel_start[None, :] + SEL_BLOCK)
           & (starts[:, None] + CMP_LEN > sel_start[None, :])
           & (np.arange(nch)[:, None] < n_cmp)).astype(np.float32)
    o_c, q_aug = _cmp_attn(q, k_cmp, v_cmp, jnp.asarray(ovl.T), tq=tq, n_cmp=n_cmp)

    ft = min(FLASH_TQ, s)
    flat = lambda a: a.reshape(b * kv, s, a.shape[-1])
    o_s = _flash(q_aug, flat(k_aug), flat(vs1), mode="causal", tq=ft, tk=ft, db=FLASH_DB, out_dtype=F32)
    o_w = _flash(q_aug, flat(kw_aug), flat(vw1), mode="window", tq=WINDOW, tk=WINDOW, out_dtype=F32)
    hm = lambda a: a.reshape(b, MIX_HEADS, s, d)
    mix = _combine(pg, hm(o_c), hm(o_s), hm(o_w), tq=tq)
    return mix, xq


def _mla_mixer(x2, b, s, w_in, q_norm, w_uq, kv_norm, w_ukv, tabs_m, tabs_mq, *, tm, tq):
    h = MIX_HEADS
    qk = MLA_NOPE + MLA_ROPE
    w_c = w_in[:, :MLA_Q_RANK + MLA_KV_RANK].astype(BF16)
    w_kr = jnp.zeros((D_MODEL, LANES), F32).at[:, :MLA_ROPE].set(
        w_in[:, MLA_Q_RANK + MLA_KV_RANK:MLA_Q_RANK + MLA_KV_RANK + MLA_ROPE]).astype(BF16)
    w_xq = w_in[:, -XA_WIDTH:].astype(BF16)
    c, kr, xq = _proj(x2, [_Out(w_c, F32), _Out(w_kr, BF16, rope=True),
                           _Out(w_xq, BF16, scale=XA_SCALE, heads=(XA_HEADS, HEAD_DIM))],
                      tm=tm, seq=s, rope=(MLA_ROPE // 2,) + tuple(tabs_m))
    wq3 = jnp.pad(w_uq.reshape(MLA_Q_RANK, h, qk), ((0, 0), (0, 0), (0, LANES - qk)))
    (q,) = _proj(c, [_Out(wq3.reshape(MLA_Q_RANK, h * LANES).astype(BF16), BF16, rope=True,
                          scale=LOG2E / math.sqrt(qk), heads=(h, LANES))],
                 tm=tm, seq=s, x_cols=(MLA_Q_RANK, 0), rms_gain=q_norm, rope=(MLA_ROPE // 2,) + tuple(tabs_mq))
    wkv3 = w_ukv.reshape(MLA_KV_RANK, h, MLA_NOPE + MLA_V)
    w_uk = wkv3[:, :, :MLA_NOPE].reshape(MLA_KV_RANK, h * MLA_NOPE).astype(BF16)
    w_uv = wkv3[:, :, MLA_NOPE:].reshape(MLA_KV_RANK, h * MLA_V).astype(BF16)
    k, v1 = _proj(c, [_Out(w_uk, BF16, heads=(h, MLA_NOPE), pad=("extra", LANES - MLA_NOPE)),
                      _Out(w_uv, BF16, heads=(h, MLA_V), pad=("ones", MLA_V))],
                  tm=tm, seq=s, x_cols=(MLA_KV_RANK, MLA_Q_RANK // MLA_KV_RANK), rms_gain=kv_norm, extra=kr)
    ft = min(FLASH_TQ, s)
    o = _flash(q.reshape(b * h, 1, s, LANES), k.reshape(b * h, s, LANES), v1.reshape(b * h, s, 2 * MLA_V),
               mode="causal", tq=ft, tk=ft, db=FLASH_DB, out_dtype=BF16)
    return o.reshape(b, h, s, MLA_V), xq


def _conv_mixer(x2, b, s, w_in, b_in, dw_w, dw_b, ln_g, ln_b, *, tm, ts):
    a, xq = _proj(x2, [_Out(w_in[:, :2 * CONV_CH].astype(BF16), F32),
                       _Out(w_in[:, 2 * CONV_CH:].astype(BF16), BF16, scale=XA_SCALE, heads=(XA_HEADS, HEAD_DIM))],
                  tm=tm, seq=s)
    mix = _conv(a.reshape(b, s, 2 * CONV_CH), b_in, dw_w, dw_b, ln_g, ln_b, ts=ts)
    return mix.reshape(b * s, CONV_CH), xq


def kernel(x, mem, positions, nsa_w_in, nsa_cmp_pe, nsa_cmp_w1, nsa_cmp_w2, mla_w_in, mla_q_norm, mla_w_uq, mla_kv_norm, mla_w_ukv, conv_w_in, conv_b_in, conv_dw_w, conv_dw_b, conv_ln_g, conv_ln_b, mem_w_kv, w_out, ln_g, ln_b, moe_w_grp, moe_b_grp, moe_w_exp, moe_b_exp, moe_w_gate, moe_w_up, moe_w_down):
    b, s, d = x.shape
    depth = w_out.shape[0]
    n_mem = mem.shape[1]
    alpha = (2.0 * depth) ** 0.25
    tm = 512
    tq = 512
    tabs_n = _rope_tables(positions, ROT_DIM, ROPE_THETA, HEAD_DIM)
    tabs_m = _rope_tables(positions, MLA_ROPE, MLA_THETA, MLA_ROPE)
    tabs_mq = _rope_tables(positions, MLA_ROPE, MLA_THETA, LANES, offset=MLA_NOPE)
    x2 = x.reshape(b * s, d)
    mem2 = mem.reshape(b * n_mem, d)
    for i in range(depth):
        kind, j = i % N_MIXERS, i // N_MIXERS
        if kind == 0:
            mix, xq = _nsa_mixer(x2, b, s, nsa_w_in[j], nsa_cmp_pe[j], nsa_cmp_w1[j], nsa_cmp_w2[j],
                                 tabs_n, tm=tm, tq=tq)
        elif kind == 1:
            mix, xq = _mla_mixer(x2, b, s, mla_w_in[j], mla_q_norm[j], mla_w_uq[j], mla_kv_norm[j],
                                 mla_w_ukv[j], tabs_m, tabs_mq, tm=tm, tq=tq)
        else:
            mix, xq = _conv_mixer(x2, b, s, conv_w_in[j], conv_b_in[j], conv_dw_w[j], conv_dw_b[j],
                                  conv_ln_g[j], conv_ln_b[j], tm=tm, ts=tq)
        xah = (XA_HEADS, HEAD_DIM)
        mk, mv1 = _proj(mem2, [_Out(mem_w_kv[i][:, :XA_WIDTH].astype(BF16), BF16, heads=xah),
                               _Out(mem_w_kv[i][:, XA_WIDTH:].astype(BF16), BF16, heads=xah,
                                    pad=("ones", HEAD_DIM))], tm=tm, seq=n_mem)
        xa = _flash(xq.reshape(b * XA_HEADS, 1, s, HEAD_DIM), mk.reshape(b * XA_HEADS, n_mem, HEAD_DIM),
                    mv1.reshape(b * XA_HEADS, n_mem, 2 * HEAD_DIM), mode="full", tq=min(FLASH_TQ, s), tk=n_mem,
                    out_dtype=BF16).reshape(b, XA_HEADS, s, HEAD_DIM)
        w_route = jnp.zeros((d, LANES), F32).at[:, :N_GROUPS].set(moe_w_grp[i])
        w_route = w_route.at[:, ROUTE_OFF:ROUTE_OFF + N_EXPERTS].set(moe_w_exp[i])
        b_route = jnp.zeros((1, LANES), F32).at[0, :N_GROUPS].set(moe_b_grp[i])
        b_route = b_route.at[0, ROUTE_OFF:ROUTE_OFF + N_EXPERTS].set(moe_b_exp[i])
        x1, gate = _out_proj(mix, xa, x2, w_out[i], ln_g[i, 0], ln_b[i, 0], w_route, b_route,
                             tm=tm, seq=s, alpha=alpha)
        x2 = _moe(x1, gate, moe_w_gate[i], moe_w_up[i], moe_w_down[i], ln_g[i, 1], ln_b[i, 1], alpha=alpha)
    return x2.reshape(b, s, d)
```

```python
import functools
import math

import numpy as np
import jax
import jax.numpy as jnp
from jax import lax
from jax.experimental import pallas as pl
from jax.experimental.pallas import tpu as pltpu

F32 = jnp.float32
BF16 = jnp.bfloat16

D_MODEL = 1024
N_MIXERS = 3
HEAD_DIM = 64
MIX_HEADS = 12
MIX_WIDTH = MIX_HEADS * HEAD_DIM
XA_HEADS = 4
XA_WIDTH = XA_HEADS * HEAD_DIM
ROPE_THETA = 500000.0
ROT_DIM = HEAD_DIM // 4
LN_EPS = 1e-5
RMS_EPS = 1e-6
NSA_KV_HEADS = 4
NSA_GROUP = MIX_HEADS // NSA_KV_HEADS
NSA_KV_WIDTH = NSA_KV_HEADS * HEAD_DIM
CMP_LEN = 32
CMP_STRIDE = 16
CMP_HIDDEN = 128
SEL_BLOCK = 64
SEL_SHIFT = 6
SEL_TOPK = 16
WINDOW = 512
MLA_Q_RANK = 256
MLA_KV_RANK = 128
MLA_NOPE = 64
MLA_ROPE = 32
MLA_V = 64
MLA_THETA = 10000.0
CONV_CH = MIX_WIDTH
CONV_WIDTH = 31
N_GROUPS = 4
EXPERTS_PER_GROUP = 4
N_EXPERTS = N_GROUPS * EXPERTS_PER_GROUP
EXPERT_FF = 512

LANES = 128
SUBLANES = 8
MXU_N = 256
VMEM_LIMIT = 56 * 1024 * 1024
NEG = -1e30
ROUTE_OFF = N_GROUPS
LOG2E = math.log2(math.e)
XA_SCALE = LOG2E / math.sqrt(HEAD_DIM)
FLASH_TQ = 1024
FLASH_DB = 512


def _cparams(*sem):
    return pltpu.CompilerParams(dimension_semantics=sem, vmem_limit_bytes=VMEM_LIMIT)


def _dot(a, b):
    return jnp.dot(a, b, preferred_element_type=F32)


def _dot_nt(a, b):
    return lax.dot_general(a, b, (((1,), (1,)), ((), ())), preferred_element_type=F32)


def _layer_norm(z, g, b):
    mu = jnp.mean(z, -1, keepdims=True)
    zc = z - mu
    var = jnp.mean(zc * zc, -1, keepdims=True)
    return zc * lax.rsqrt(var + LN_EPS) * g + b


def _sigmoid(x):
    return 1.0 / (1.0 + jnp.exp(-x))


def _rope(acc, r, c_ref, s1_ref, s2_ref):
    n = acc.shape[1]
    rep = n // LANES
    c = jnp.tile(c_ref[...], (1, rep))
    s1 = jnp.tile(s1_ref[...], (1, rep))
    s2 = jnp.tile(s2_ref[...], (1, rep))
    return acc * c + pltpu.roll(acc, n - r, 1) * s1 + pltpu.roll(acc, r, 1) * s2


class _Out:
    def __init__(self, w, dtype, rope=False, scale=1.0, heads=None, pad=None):
        self.w, self.dtype, self.rope, self.scale, self.heads, self.pad = w, dtype, rope, scale, heads, pad


def _proj_kernel(*refs, outs, rms, rope_r, has_extra, tm, spt):
    it = iter(refs)
    x_ref = next(it)
    g_ref = next(it) if rms else None
    tabs = (next(it), next(it), next(it)) if rope_r else None
    extra_ref = next(it) if has_extra else None
    w_refs = [next(it) for _ in outs]
    o_refs = [next(it) for _ in outs]
    xv = x_ref[...]
    if rms:
        xf = xv.astype(F32)
        xf = xf * lax.rsqrt(jnp.mean(xf * xf, -1, keepdims=True) + RMS_EPS) * g_ref[...]
        xb = xf.astype(BF16)
    else:
        xb = xv.astype(BF16)

    def pad_tile(kind, width):
        if kind == "onehot":
            tok = (pl.program_id(0) % spt) * tm + lax.broadcasted_iota(jnp.int32, (tm, width), 0)
            blk = lax.broadcasted_iota(jnp.int32, (tm, width), 1)
            return jnp.where(jnp.right_shift(tok, SEL_SHIFT) == blk, 1.0, 0.0)
        if kind == "extra":
            return extra_ref[:, :width].astype(F32)
        return jnp.full((tm, width), 1.0 if kind == "ones" else 0.0, F32)

    for o, w_ref, o_ref in zip(outs, w_refs, o_refs):
        n = w_ref.shape[1]
        step = MXU_N if n % MXU_N == 0 else LANES
        pad = pad_tile(*o.pad) if o.pad else None
        for c0 in range(0, n, step):
            acc = _dot(xb, w_ref[:, c0:c0 + step])
            if o.rope:
                acc = _rope(acc, rope_r, *tabs)
            if o.scale != 1.0:
                acc = acc * o.scale
            if o.heads is None:
                o_ref[:, c0:c0 + step] = acc.astype(o_ref.dtype)
            else:
                hw = o.heads[1]
                for j in range(step // hw):
                    piece = acc[:, j * hw:(j + 1) * hw]
                    if pad is not None:
                        piece = jnp.concatenate([piece, pad], axis=1)
                    o_ref[0, c0 // hw + j] = piece.astype(o_ref.dtype)


def _proj(x, outs, *, tm, seq, x_cols=None, rms_gain=None, rope=None, extra=None):
    m = x.shape[0]
    tm = min(tm, seq)
    spt = seq // tm
    kdim, xj = x_cols if x_cols else (x.shape[1], 0)
    ins = [x]
    in_specs = [pl.BlockSpec((tm, kdim), lambda i: (i, xj))]
    if rms_gain is not None:
        ins.append(rms_gain.reshape(1, kdim).astype(F32))
        in_specs.append(pl.BlockSpec((1, kdim), lambda i: (0, 0)))
    rope_r = 0
    if rope is not None:
        rope_r = rope[0]
        for t in rope[1:]:
            ins.append(t)
            in_specs.append(pl.BlockSpec((tm, LANES), lambda i: (i, 0)))
    if extra is not None:
        ins.append(extra)
        in_specs.append(pl.BlockSpec((tm, LANES), lambda i: (i, 0)))
    for o in outs:
        ins.append(o.w)
        in_specs.append(pl.BlockSpec(o.w.shape, lambda i: (0, 0)))
    out_shape, out_specs = [], []
    for o in outs:
        if o.heads is None:
            out_shape.append(jax.ShapeDtypeStruct((m, o.w.shape[1]), o.dtype))
            out_specs.append(pl.BlockSpec((tm, o.w.shape[1]), lambda i: (i, 0)))
        else:
            nh, hw = o.heads
            assert nh * hw == o.w.shape[1]
            width = hw + (o.pad[1] if o.pad else 0)
            out_shape.append(jax.ShapeDtypeStruct((m // seq, nh, seq, width), o.dtype))
            out_specs.append(pl.BlockSpec((1, nh, tm, width), lambda i: (i // spt, 0, i % spt, 0)))
    kern = functools.partial(_proj_kernel, outs=tuple(outs), rms=rms_gain is not None, rope_r=rope_r,
                             has_extra=extra is not None, tm=tm, spt=spt)
    return pl.pallas_call(kern, grid=(m // tm,), in_specs=in_specs, out_specs=out_specs,
                          out_shape=out_shape, compiler_params=_cparams("parallel"))(*ins)


def _flash_kernel(q_ref, k_ref, v_ref, o_ref, m_ref, acc_ref, *, mode, g, tq, tk, db):
    i = pl.program_id(1)
    dv = o_ref.shape[-1]
    m_ref[...] = jnp.full(m_ref.shape, NEG, F32)
    acc_ref[...] = jnp.zeros(acc_ref.shape, F32)

    def block(h, r0, nr, kstart, nk, kind):
        kt = k_ref[0, pl.ds(kstart, nk), :]
        vt = v_ref[0, pl.ds(kstart, nk), :]
        s = _dot_nt(q_ref[0, h, r0:r0 + nr, :], kt)
        if kind is not None:
            row = lax.broadcasted_iota(jnp.int32, (nr, nk), 0)
            col = lax.broadcasted_iota(jnp.int32, (nr, nk), 1)
            s = jnp.where(col <= row if kind == "diag" else col > row, s, NEG)
        a0 = h * tq + r0
        m_prev = m_ref[a0:a0 + nr, :]
        m_new = jnp.maximum(m_prev, jnp.max(s, -1, keepdims=True))
        alpha = jnp.exp2(m_prev - m_new)
        p = jnp.exp2(s - jnp.tile(m_new, (1, nk // LANES)))
        acc_ref[a0:a0 + nr, :] = alpha * acc_ref[a0:a0 + nr, :] + _dot(p.astype(BF16), vt)
        m_ref[a0:a0 + nr, :] = m_new

    if mode == "full":
        for j in range(k_ref.shape[1] // tk):
            for h in range(g):
                block(h, 0, tq, j * tk, tk, None)
    elif mode == "window":
        t0 = pl.multiple_of(i * tq, tq)
        for h in range(g):
            block(h, 0, tq, t0, tq, "diag")

        @pl.when(i > 0)
        def _():
            for h in range(g):
                block(h, 0, tq, pl.multiple_of(t0 - tq, tq), tq, "prev")
    else:
        t0 = pl.multiple_of(i * tq, tq)
        for a in range(tq // db):
            for h in range(g):
                block(h, a * db, db, pl.multiple_of(t0 + a * db, db), db, "diag")
                for c in range(a):
                    block(h, a * db, db, pl.multiple_of(t0 + c * db, db), db, None)

        def body(j, carry):
            for h in range(g):
                block(h, 0, tq, pl.multiple_of(j * tk, tk), tk, None)
            return carry
        lax.fori_loop(0, i * (tq // tk), body, 0)

    acc = acc_ref[...]
    o = acc / pltpu.roll(acc, dv, 1)
    o_ref[0] = o[:, :dv].reshape(g, tq, dv).astype(o_ref.dtype)


def _flash(q, k, v_ones, *, mode, tq, tk, out_dtype, db=None):
    n, g, s, dk = q.shape
    sk, dv = v_ones.shape[1], v_ones.shape[2] // 2
    assert 2 * dv == LANES
    kern = functools.partial(_flash_kernel, mode=mode, g=g, tq=tq, tk=tk, db=db or tq)
    return pl.pallas_call(
        kern, grid=(n, s // tq),
        in_specs=[pl.BlockSpec((1, g, tq, dk), lambda b, i: (b, 0, i, 0)),
                  pl.BlockSpec((1, sk, dk), lambda b, i: (b, 0, 0)),
                  pl.BlockSpec((1, sk, 2 * dv), lambda b, i: (b, 0, 0))],
        out_specs=pl.BlockSpec((1, g, tq, dv), lambda b, i: (b, 0, i, 0)),
        out_shape=jax.ShapeDtypeStruct((n, g, s, dv), out_dtype),
        scratch_shapes=[pltpu.VMEM((g * tq, LANES), F32), pltpu.VMEM((g * tq, 2 * dv), F32)],
        compiler_params=_cparams("parallel", "arbitrary"))(q, k, v_ones)


def _cmp_kernel(x_ref, w1s_ref, pe_ref, w1_ref, w2_ref, c_ref, s1_ref, s2_ref, o_ref, *, rope):
    nch = x_ref.shape[1]
    ab = _dot(x_ref[0], w1s_ref[...])
    pe_term = _dot(pe_ref[...], w1_ref[...])[0:1, :]
    nxt = pltpu.roll(ab[:, CMP_HIDDEN:], nch - 1, 0)
    h = ab[:, :CMP_HIDDEN] + nxt + pe_term
    h = h * _sigmoid(h)
    o = _dot(h.astype(BF16), w2_ref[...])
    if rope:
        o = _rope(o, ROT_DIM // 2, c_ref.at[0], s1_ref.at[0], s2_ref.at[0])
    o_ref[0] = o[:, :HEAD_DIM].astype(o_ref.dtype)


def _compress(tc, pe, w1, w2, tabs, *, rope, kvh):
    n, nch, width = tc.shape
    half = CMP_STRIDE * HEAD_DIM
    w1b = w1.astype(BF16)
    w1s = jnp.concatenate([w1b[:half], w1b[half:]], axis=1)
    pe8 = jnp.zeros((8, CMP_LEN * HEAD_DIM), BF16).at[0].set(pe.reshape(-1).astype(BF16))
    w2p = jnp.zeros((CMP_HIDDEN, LANES), BF16).at[:, :HEAD_DIM].set(w2.astype(BF16))
    kern = functools.partial(_cmp_kernel, rope=rope)
    tab_spec = pl.BlockSpec((1, nch, LANES), lambda i: (i // kvh, 0, 0))
    return pl.pallas_call(
        kern, grid=(n,),
        in_specs=[pl.BlockSpec((1, nch, width), lambda i: (i, 0, 0)),
                  pl.BlockSpec(w1s.shape, lambda i: (0, 0)),
                  pl.BlockSpec(pe8.shape, lambda i: (0, 0)),
                  pl.BlockSpec(w1b.shape, lambda i: (0, 0)),
                  pl.BlockSpec(w2p.shape, lambda i: (0, 0)),
                  tab_spec, tab_spec, tab_spec],
        out_specs=pl.BlockSpec((1, nch, HEAD_DIM), lambda i: (i, 0, 0)),
        out_shape=jax.ShapeDtypeStruct((n, nch, HEAD_DIM), BF16),
        compiler_params=_cparams("parallel"))(tc, w1s, pe8, w1b, w2p, *tabs)


def _cmp_attn_kernel(q_ref, kc_ref, vc_ref, ovl_ref, oc_ref, qa_ref, *, g, tq, n_cmp, scale):
    i = pl.program_id(1)
    nch = kc_ref.shape[1]
    n_sel = ovl_ref.shape[0]
    kc = kc_ref[0]
    vc = vc_ref[0]
    t0 = i * tq
    t_lane = t0 + lax.broadcasted_iota(jnp.int32, (nch, tq), 1)
    n_sub = lax.broadcasted_iota(jnp.int32, (nch, tq), 0)
    vis_t = (n_sub < n_cmp) & (n_sub * CMP_STRIDE + (CMP_LEN - 1) <= t_lane)
    p_sum_t = jnp.zeros((nch, tq), F32)
    for h in range(g):
        qh = q_ref[0, h]
        st = jnp.where(vis_t, _dot_nt(kc, qh) * scale, NEG)
        mt = jnp.max(st, 0, keepdims=True)
        et = jnp.where(vis_t, jnp.exp(st - mt), 0.0)
        pt = et / jnp.maximum(jnp.sum(et, 0, keepdims=True), 1e-30)
        p_sum_t = p_sum_t + pt
        oc_ref[0, h] = lax.dot_general(pt.astype(BF16), vc, (((0,), (0,)), ((), ())),
                                       preferred_element_type=F32).astype(oc_ref.dtype)
    imp = jnp.dot(ovl_ref[...], p_sum_t, preferred_element_type=F32,
                  precision=lax.Precision.HIGHEST)
    t1 = t0 + lax.broadcasted_iota(jnp.int32, (n_sel, tq), 1)
    blk = lax.broadcasted_iota(jnp.int32, (n_sel, tq), 0)
    cur = jnp.right_shift(t1, SEL_SHIFT)
    valid = blk <= cur
    forced = (blk == 0) | (blk == cur) | (blk == cur - 1)
    imp = jnp.where(valid, jnp.where(forced, jnp.inf, imp), -jnp.inf)
    nv = n_sel // SUBLANES
    groups = [imp[v * SUBLANES:(v + 1) * SUBLANES] for v in range(nv)]
    sub = lax.broadcasted_iota(jnp.int32, (SUBLANES, tq), 0)

    def count_group(rb, ranks):
        ranks = list(ranks)
        for rr in range(SUBLANES):
            row = groups[rb][rr:rr + 1, :]
            for v in range(nv):
                if v > rb:
                    ahead = jnp.where(row >= groups[v], 1.0, 0.0)
                elif v < rb:
                    ahead = jnp.where(row > groups[v], 1.0, 0.0)
                else:
                    ahead = jnp.where(sub > rr, jnp.where(row >= groups[v], 1.0, 0.0),
                                      jnp.where(row > groups[v], 1.0, 0.0))
                ranks[v] = ranks[v] + ahead
        return tuple(ranks)

    last_group = jnp.right_shift(t0 + (tq - 1), SEL_SHIFT) // SUBLANES
    ranks = tuple(jnp.zeros((SUBLANES, tq), F32) for _ in range(nv))
    for rb in range(nv):
        ranks = lax.cond(rb <= last_group, functools.partial(count_group, rb), lambda rk: rk, ranks)
    rank = jnp.concatenate(ranks, axis=0)
    chosen_t = jnp.where((rank < float(min(SEL_TOPK, n_sel))) & valid, 1.0, 0.0).astype(BF16)
    eye = (lax.broadcasted_iota(jnp.int32, (tq, tq), 0)
           == lax.broadcasted_iota(jnp.int32, (tq, tq), 1))
    chosen = _dot_nt(jnp.where(eye, 1.0, 0.0).astype(BF16), chosen_t)
    bias = (chosen - 1.0) * (-NEG)
    for h in range(g):
        qs = q_ref[0, h].astype(F32) * (scale * LOG2E)
        qa_ref[0, h] = jnp.concatenate([qs, bias], axis=1).astype(qa_ref.dtype)


def _cmp_attn(q, kc, vc, ovl_t, *, tq, n_cmp):
    n, g, s, d = q.shape
    nch = kc.shape[1]
    n_sel = ovl_t.shape[0]
    kern = functools.partial(_cmp_attn_kernel, g=g, tq=tq, n_cmp=n_cmp, scale=1.0 / math.sqrt(HEAD_DIM))
    return pl.pallas_call(
        kern, grid=(n, s // tq),
        in_specs=[pl.BlockSpec((1, g, tq, d), lambda b, i: (b, 0, i, 0)),
                  pl.BlockSpec((1, nch, d), lambda b, i: (b, 0, 0)),
                  pl.BlockSpec((1, nch, d), lambda b, i: (b, 0, 0)),
                  pl.BlockSpec(ovl_t.shape, lambda b, i: (0, 0))],
        out_specs=[pl.BlockSpec((1, g, tq, d), lambda b, i: (b, 0, i, 0)),
                   pl.BlockSpec((1, g, tq, d + n_sel), lambda b, i: (b, 0, i, 0))],
        out_shape=[jax.ShapeDtypeStruct((n, g, s, d), F32),
                   jax.ShapeDtypeStruct((n, g, s, d + n_sel), BF16)],
        compiler_params=_cparams("parallel", "arbitrary"))(q, kc, vc, ovl_t)


def _combine_kernel(gl_ref, oc_ref, os_ref, ow_ref, o_ref, *, nh):
    gates = _sigmoid(gl_ref[...])
    for h in range(nh):
        o = (gates[:, 3 * h:3 * h + 1] * oc_ref[0, h] + gates[:, 3 * h + 1:3 * h + 2] * os_ref[0, h]
             + gates[:, 3 * h + 2:3 * h + 3] * ow_ref[0, h])
        o_ref[0, h] = o.astype(o_ref.dtype)


def _combine(gl, oc, osel, ow, *, tq):
    bsz, nh, s, d = oc.shape
    spt = s // tq
    spec = pl.BlockSpec((1, nh, tq, d), lambda b, i: (b, 0, i, 0))
    return pl.pallas_call(
        functools.partial(_combine_kernel, nh=nh), grid=(bsz, spt),
        in_specs=[pl.BlockSpec((tq, LANES), lambda b, i: (b * spt + i, 0)), spec, spec, spec],
        out_specs=spec, out_shape=jax.ShapeDtypeStruct((bsz, nh, s, d), BF16),
        compiler_params=_cparams("parallel", "parallel"))(gl, oc, osel, ow)


CONV_HALO = 32
CONV_ROWS = 64


def _conv_kernel(cur_ref, prev_ref, bin_ref, w_ref, wb_ref, g_ref, b_ref, o_ref, u_ref, al_ref, *, ts):
    i = pl.program_id(1)
    ch = o_ref.shape[-1]

    def glu(a):
        a = a + bin_ref[...]
        return a[:, :ch] * _sigmoid(a[:, ch:])

    u_ref[CONV_HALO:CONV_HALO + ts, :] = glu(cur_ref[0])

    @pl.when(i == 0)
    def _():
        u_ref[0:CONV_HALO, :] = jnp.zeros((CONV_HALO, ch), F32)

    @pl.when(i > 0)
    def _():
        u_ref[0:CONV_HALO, :] = glu(prev_ref[0])

    base = CONV_HALO - (CONV_WIDTH - 1)
    for r0 in range(0, ts, CONV_ROWS):
        acc = jnp.zeros((CONV_ROWS, ch), F32)
        for sft in range(SUBLANES):
            taps = [k for k in range(CONV_WIDTH) if (base + k) % SUBLANES == sft]
            lo = base + taps[0]
            span = taps[-1] - taps[0] + CONV_ROWS
            al_ref[0:span, :] = u_ref[r0 + lo:r0 + lo + span, :]
            for k in taps:
                off = base + k - lo
                acc = acc + al_ref[off:off + CONV_ROWS, :] * w_ref[k:k + 1, :]
        y = _layer_norm(acc + wb_ref[...], g_ref[...], b_ref[...])
        o_ref[0, r0:r0 + CONV_ROWS, :] = (y * _sigmoid(y)).astype(o_ref.dtype)


def _conv(a, b_in, dw_w, dw_b, ln_g, ln_b, *, ts):
    bsz, s, two_ch = a.shape
    ch = two_ch // 2
    hb = ts // CONV_HALO
    row = lambda v: v.reshape(1, -1).astype(F32)
    return pl.pallas_call(
        functools.partial(_conv_kernel, ts=ts), grid=(bsz, s // ts),
        in_specs=[pl.BlockSpec((1, ts, two_ch), lambda b, i: (b, i, 0)),
                  pl.BlockSpec((1, CONV_HALO, two_ch), lambda b, i: (b, jnp.maximum(i * hb - 1, 0), 0)),
                  pl.BlockSpec((1, two_ch), lambda b, i: (0, 0)),
                  pl.BlockSpec((CONV_WIDTH, ch), lambda b, i: (0, 0)),
                  pl.BlockSpec((1, ch), lambda b, i: (0, 0)),
                  pl.BlockSpec((1, ch), lambda b, i: (0, 0)),
                  pl.BlockSpec((1, ch), lambda b, i: (0, 0))],
        out_specs=pl.BlockSpec((1, ts, ch), lambda b, i: (b, i, 0)),
        out_shape=jax.ShapeDtypeStruct((bsz, s, ch), BF16),
        scratch_shapes=[pltpu.VMEM((CONV_HALO + ts, ch), F32), pltpu.VMEM((CONV_HALO + CONV_ROWS, ch), F32)],
        compiler_params=_cparams("parallel", "arbitrary"))(
            a, a, row(b_in), dw_w.astype(F32), row(dw_b), row(ln_g), row(ln_b))


def _out_kernel(mix_ref, xa_ref, x_ref, w_ref, g_ref, b_ref, wr_ref, br_ref, x1_ref, gate_ref,
                cat_ref, *, alpha, mix_heads):
    if mix_heads:
        for h in range(MIX_HEADS):
            cat_ref[:, h * HEAD_DIM:(h + 1) * HEAD_DIM] = mix_ref[0, h].astype(F32)
    else:
        cat_ref[:, :MIX_WIDTH] = mix_ref[...].astype(F32)
    for h in range(XA_HEADS):
        cat_ref[:, MIX_WIDTH + h * HEAD_DIM:MIX_WIDTH + (h + 1) * HEAD_DIM] = xa_ref[0, h].astype(F32)
    y = _dot(cat_ref[...].astype(BF16), w_ref[...])
    x1 = _layer_norm(alpha * x_ref[...] + y, g_ref[...], b_ref[...])
    x1_ref[...] = x1
    hi = x1.astype(BF16)
    lo = (x1 - hi.astype(F32)).astype(BF16)
    wr = wr_ref[...]
    w_hi = wr.astype(BF16)
    w_lo = (wr - w_hi.astype(F32)).astype(BF16)
    logits = ((_dot(hi, w_hi) + _dot(lo, w_hi)) + (_dot(hi, w_lo) + _dot(lo, w_lo)) + br_ref[...])
    lane = lax.broadcasted_iota(jnp.int32, logits.shape, 1)
    far = jnp.int32(LANES)
    lg = jnp.where(lane < N_GROUPS, logits, -jnp.inf)
    mg = jnp.max(lg, -1, keepdims=True)
    g_prob = 1.0 / jnp.sum(jnp.exp(lg - mg), -1, keepdims=True)
    g_sel = jnp.min(jnp.where(lg == mg, lane, far), -1, keepdims=True)
    lo = ROUTE_OFF + EXPERTS_PER_GROUP * g_sel
    le = jnp.where((lane >= lo) & (lane < lo + EXPERTS_PER_GROUP), logits, -jnp.inf)
    m1 = jnp.max(le, -1, keepdims=True)
    i1 = jnp.min(jnp.where(le == m1, lane, far), -1, keepdims=True)
    le2 = jnp.where(lane == i1, -jnp.inf, le)
    m2 = jnp.max(le2, -1, keepdims=True)
    i2 = jnp.min(jnp.where(le2 == m2, lane, far), -1, keepdims=True)
    e2 = jnp.exp(m2 - m1)
    p1 = g_prob / (1.0 + e2)
    gates = jnp.where(lane == i1, p1, jnp.where(lane == i2, p1 * e2, 0.0))
    gate_ref[...] = jnp.where(lane == 0, g_sel.astype(F32), gates)


def _out_proj(mix, xa, x, w_out, ln_g, ln_b, w_route, b_route, *, tm, seq, alpha):
    t, d = x.shape
    tm = min(tm, seq)
    spt = seq // tm
    mix_heads = mix.ndim == 4
    row = lambda v: v.reshape(1, -1).astype(F32)
    full = lambda a: pl.BlockSpec(a.shape, lambda i: (0, 0))
    heads = lambda nh: pl.BlockSpec((1, nh, tm, HEAD_DIM), lambda i: (i // spt, 0, i % spt, 0))
    ins = [mix, xa, x, w_out.astype(BF16), row(ln_g), row(ln_b), w_route, b_route]
    in_specs = [heads(MIX_HEADS) if mix_heads else pl.BlockSpec((tm, MIX_WIDTH), lambda i: (i, 0)),
                heads(XA_HEADS),
                pl.BlockSpec((tm, d), lambda i: (i, 0))] + [full(a) for a in ins[3:]]
    return pl.pallas_call(
        functools.partial(_out_kernel, alpha=alpha, mix_heads=mix_heads), grid=(t // tm,), in_specs=in_specs,
        out_specs=[pl.BlockSpec((tm, d), lambda i: (i, 0)), pl.BlockSpec((tm, LANES), lambda i: (i, 0))],
        out_shape=[jax.ShapeDtypeStruct((t, d), F32), jax.ShapeDtypeStruct((t, LANES), F32)],
        scratch_shapes=[pltpu.VMEM((tm, MIX_WIDTH + XA_WIDTH), F32)],
        compiler_params=_cparams("parallel"))(*ins)


MOE_CHUNK = 2048
MOE_TILE = 256
MOE_UNROLL = 8


def _moe_kernel(order_ref, meta_ref, x_ref, gate_ref, wg_ref, wu_ref, wd_ref, g_ref, b_ref, o_ref,
                xs_ref, gs_ref, ys_ref, tmp_ref, *, alpha, ch, tile):
    e = pl.program_id(1)
    grp = e // EXPERTS_PER_GROUP
    k = e % EXPERTS_PER_GROUP
    start = meta_ref[0, 0, grp]
    n_rows = meta_ref[0, 0, N_GROUPS + grp]
    n_tiles = (n_rows + (tile - 1)) // tile

    def token(pos):
        return order_ref[0, 0, start + pos]

    @pl.when(k == 0)
    def _():
        def gather_tile(rt, carry):
            r0 = pl.multiple_of(rt * tile, tile)

            def rows(blk, c2):
                i0 = pl.multiple_of(blk * MOE_UNROLL, MOE_UNROLL)
                g0 = pl.multiple_of(r0 + i0, MOE_UNROLL)
                for u in range(MOE_UNROLL):
                    t = token(g0 + u)
                    tmp_ref[pl.ds(i0 + u, 1), :] = x_ref[pl.ds(t, 1), :]
                    gs_ref[pl.ds(g0 + u, 1), :] = gate_ref[pl.ds(t, 1), :]
                return c2
            lax.fori_loop(0, tile // MOE_UNROLL, rows, 0)
            xs_ref[pl.ds(r0, tile), :] = tmp_ref[...].astype(BF16)
            return carry
        lax.fori_loop(0, n_tiles, gather_tile, 0)

    lane = lax.broadcasted_iota(jnp.int32, (tile, LANES), 1)

    def expert_tile(rt, carry):
        r0 = pl.multiple_of(rt * tile, tile)
        xb = xs_ref[pl.ds(r0, tile), :]
        gcol = jnp.sum(jnp.where(lane == ROUTE_OFF + e, gs_ref[pl.ds(r0, tile), :], 0.0), -1, keepdims=True)
        hg = _dot(xb, wg_ref[0])
        h = hg * _sigmoid(hg) * _dot(xb, wu_ref[0]) * gcol
        y = _dot(h.astype(BF16), wd_ref[0])

        @pl.when(k == 0)
        def _():
            ys_ref[pl.ds(r0, tile), :] = y

        @pl.when(k > 0)
        def _():
            ys_ref[pl.ds(r0, tile), :] += y
        return carry
    lax.fori_loop(0, n_tiles, expert_tile, 0)

    @pl.when(k == EXPERTS_PER_GROUP - 1)
    def _():
        def rows(blk, c2):
            i0 = pl.multiple_of(blk * MOE_UNROLL, MOE_UNROLL)
            for u in range(MOE_UNROLL):
                o_ref[pl.ds(token(i0 + u), 1), :] = ys_ref[pl.ds(i0 + u, 1), :]
            return c2
        lax.fori_loop(0, n_tiles * (tile // MOE_UNROLL), rows, 0)

    @pl.when(e == pl.num_programs(1) - 1)
    def _():
        for r0 in range(0, ch, tile):
            z = alpha * x_ref[r0:r0 + tile, :] + o_ref[r0:r0 + tile, :]
            o_ref[r0:r0 + tile, :] = _layer_norm(z, g_ref[...], b_ref[...])


def _moe(x1, gate, wg, wu, wd, ln_g, ln_b, *, alpha):
    t, d = x1.shape
    ne, _, ff = wg.shape
    ch = min(MOE_CHUNK, t)
    nc = t // ch
    gsel = gate[:, 0].astype(jnp.int32).reshape(nc, ch)
    order = jnp.argsort(gsel, axis=1, stable=True).astype(jnp.int32)
    order = jnp.concatenate([order, jnp.broadcast_to(order[:, -1:], (nc, MOE_TILE))], axis=1)
    counts = jnp.sum(gsel[:, :, None] == jnp.arange(N_GROUPS, dtype=jnp.int32), axis=1, dtype=jnp.int32)
    starts = jnp.cumsum(counts, axis=1, dtype=jnp.int32) - counts
    meta = jnp.concatenate([starts, counts], axis=1)
    row = lambda v: v.reshape(1, -1).astype(F32)
    smem = lambda n: pl.BlockSpec((1, 1, n), lambda c, e: (c, 0, 0), memory_space=pltpu.SMEM)
    return pl.pallas_call(
        functools.partial(_moe_kernel, alpha=alpha, ch=ch, tile=MOE_TILE), grid=(nc, ne),
        in_specs=[smem(ch + MOE_TILE), smem(2 * N_GROUPS),
                  pl.BlockSpec((ch, d), lambda c, e: (c, 0), pipeline_mode=pl.Buffered(1)),
                  pl.BlockSpec((ch, LANES), lambda c, e: (c, 0)),
                  pl.BlockSpec((1, d, ff), lambda c, e: (e, 0, 0)),
                  pl.BlockSpec((1, d, ff), lambda c, e: (e, 0, 0)),
                  pl.BlockSpec((1, ff, d), lambda c, e: (e, 0, 0)),
                  pl.BlockSpec((1, d), lambda c, e: (0, 0)),
                  pl.BlockSpec((1, d), lambda c, e: (0, 0))],
        out_specs=pl.BlockSpec((ch, d), lambda c, e: (c, 0)),
        out_shape=jax.ShapeDtypeStruct((t, d), F32),
        scratch_shapes=[pltpu.VMEM((ch, d), BF16), pltpu.VMEM((ch, LANES), F32), pltpu.VMEM((ch, d), F32),
                        pltpu.VMEM((MOE_TILE, d), F32)],
        compiler_params=_cparams("parallel", "arbitrary"))(
            order.reshape(nc, 1, ch + MOE_TILE), meta.reshape(nc, 1, 2 * N_GROUPS), x1, gate,
            wg.astype(BF16), wu.astype(BF16), wd.astype(BF16), row(ln_g), row(ln_b))


def _rope_tables(positions, dim, theta, period, offset=0):
    b, s = positions.shape
    half = dim // 2
    inv = theta ** (-jnp.arange(0, dim, 2, dtype=F32) / dim)
    ang = positions.astype(F32)[..., None] * inv
    cos, sin = jnp.cos(ang), jnp.sin(ang)
    rest = period - dim - offset
    fill = lambda v, n: jnp.full((b, s, n), v, F32)
    c = jnp.concatenate([fill(1.0, offset), cos, cos, fill(1.0, rest)], -1)
    s1 = jnp.concatenate([fill(0.0, offset), -sin, fill(0.0, half + rest)], -1)
    s2 = jnp.concatenate([fill(0.0, offset + half), sin, fill(0.0, rest)], -1)
    rep = LANES // period
    return tuple(jnp.tile(t, (1, 1, rep)).reshape(b * s, LANES) for t in (c, s1, s2))


def _nsa_mixer(x2, b, s, w_in, cmp_pe, cmp_w1, cmp_w2, tabs_n, *, tm, tq):
    kv, g, d = NSA_KV_HEADS, NSA_GROUP, HEAD_DIM
    offs = np.cumsum([0, MIX_WIDTH] + [NSA_KV_WIDTH] * 6 + [3 * MIX_HEADS, XA_WIDTH]).tolist()
    col = lambda k: w_in[:, offs[k]:offs[k + 1]]
    wq, wkc, wvc, wks, wvs, wkw, wvw, wgt, wxq = [col(k) for k in range(9)]
    w_gate = jnp.zeros((D_MODEL, LANES), F32).at[:, :3 * MIX_HEADS].set(wgt).astype(BF16)
    n_sel = s // SEL_BLOCK
    bf = lambda w: w.astype(BF16)
    kvh = (kv, d)
    q, k_aug, kw_aug, kc, vc, vs1, vw1, pg, xq = _proj(
        x2, [_Out(bf(wq), BF16, rope=True, heads=(MIX_HEADS, d)),
             _Out(bf(wks), BF16, rope=True, heads=kvh, pad=("onehot", n_sel)),
             _Out(bf(wkw), BF16, rope=True, heads=kvh, pad=("zeros", n_sel)),
             _Out(bf(wkc), BF16, heads=kvh), _Out(bf(wvc), BF16, heads=kvh),
             _Out(bf(wvs), BF16, heads=kvh, pad=("ones", d)), _Out(bf(wvw), BF16, heads=kvh, pad=("ones", d)),
             _Out(w_gate, F32), _Out(bf(wxq), BF16, scale=XA_SCALE, heads=(XA_HEADS, d))],
        tm=tm, seq=s, rope=(ROT_DIM // 2,) + tuple(tabs_n))
    q = q.reshape(b * kv, g, s, d)
    nch = s // CMP_STRIDE
    kc = kc.reshape(b * kv, nch, CMP_STRIDE * d)
    vc = vc.reshape(b * kv, nch, CMP_STRIDE * d)

    n_cmp = (s - CMP_LEN) // CMP_STRIDE + 1
    end_tabs = tuple(t.reshape(b, s, LANES)[:, CMP_LEN - 1::CMP_STRIDE][:, :nch] for t in tabs_n)
    end_tabs = tuple(jnp.pad(t, ((0, 0), (0, nch - t.shape[1]), (0, 0))) for t in end_tabs)
    k_cmp = _compress(kc, cmp_pe[0], cmp_w1[0], cmp_w2[0], end_tabs, rope=True, kvh=kv)
    v_cmp = _compress(vc, cmp_pe[1], cmp_w1[1], cmp_w2[1], end_tabs, rope=False, kvh=kv)

    starts = np.arange(nch) * CMP_STRIDE
    sel_start = np.arange(n_sel) * SEL_BLOCK
    ovl = ((starts[:, None] < sel_start[None, :] + SEL_BLOCK)
           & (starts[:, None] + CMP_LEN > sel_start[None, :])
           & (np.arange(nch)[:, None] < n_cmp)).astype(np.float32)
    o_c, q_aug = _cmp_attn(q, k_cmp, v_cmp, jnp.asarray(ovl.T), tq=tq, n_cmp=n_cmp)

    ft = min(FLASH_TQ, s)
    flat = lambda a: a.reshape(b * kv, s, a.shape[-1])
    o_s = _flash(q_aug, flat(k_aug), flat(vs1), mode="causal", tq=ft, tk=ft, db=FLASH_DB, out_dtype=F32)
    o_w = _flash(q_aug, flat(kw_aug), flat(vw1), mode="window", tq=WINDOW, tk=WINDOW, out_dtype=F32)
    hm = lambda a: a.reshape(b, MIX_HEADS, s, d)
    mix = _combine(pg, hm(o_c), hm(o_s), hm(o_w), tq=tq)
    return mix, xq


def _mla_mixer(x2, b, s, w_in, q_norm, w_uq, kv_norm, w_ukv, tabs_m, tabs_mq, *, tm, tq):
    h = MIX_HEADS
    qk = MLA_NOPE + MLA_ROPE
    w_c = w_in[:, :MLA_Q_RANK + MLA_KV_RANK].astype(BF16)
    w_kr = jnp.zeros((D_MODEL, LANES), F32).at[:, :MLA_ROPE].set(
        w_in[:, MLA_Q_RANK + MLA_KV_RANK:MLA_Q_RANK + MLA_KV_RANK + MLA_ROPE]).astype(BF16)
    w_xq = w_in[:, -XA_WIDTH:].astype(BF16)
    c, kr, xq = _proj(x2, [_Out(w_c, F32), _Out(w_kr, BF16, rope=True),
                           _Out(w_xq, BF16, scale=XA_SCALE, heads=(XA_HEADS, HEAD_DIM))],
                      tm=tm, seq=s, rope=(MLA_ROPE // 2,) + tuple(tabs_m))
    wq3 = jnp.pad(w_uq.reshape(MLA_Q_RANK, h, qk), ((0, 0), (0, 0), (0, LANES - qk)))
    (q,) = _proj(c, [_Out(wq3.reshape(MLA_Q_RANK, h * LANES).astype(BF16), BF16, rope=True,
                          scale=LOG2E / math.sqrt(qk), heads=(h, LANES))],
                 tm=tm, seq=s, x_cols=(MLA_Q_RANK, 0), rms_gain=q_norm, rope=(MLA_ROPE // 2,) + tuple(tabs_mq))
    wkv3 = w_ukv.reshape(MLA_KV_RANK, h, MLA_NOPE + MLA_V)
    w_uk = wkv3[:, :, :MLA_NOPE].reshape(MLA_KV_RANK, h * MLA_NOPE).astype(BF16)
    w_uv = wkv3[:, :, MLA_NOPE:].reshape(MLA_KV_RANK, h * MLA_V).astype(BF16)
    k, v1 = _proj(c, [_Out(w_uk, BF16, heads=(h, MLA_NOPE), pad=("extra", LANES - MLA_NOPE)),
                      _Out(w_uv, BF16, heads=(h, MLA_V), pad=("ones", MLA_V))],
                  tm=tm, seq=s, x_cols=(MLA_KV_RANK, MLA_Q_RANK // MLA_KV_RANK), rms_gain=kv_norm, extra=kr)
    ft = min(FLASH_TQ, s)
    o = _flash(q.reshape(b * h, 1, s, LANES), k.reshape(b * h, s, LANES), v1.reshape(b * h, s, 2 * MLA_V),
               mode="causal", tq=ft, tk=ft, db=FLASH_DB, out_dtype=BF16)
    return o.reshape(b, h, s, MLA_V), xq


def _conv_mixer(x2, b, s, w_in, b_in, dw_w, dw_b, ln_g, ln_b, *, tm, ts):
    a, xq = _proj(x2, [_Out(w_in[:, :2 * CONV_CH].astype(BF16), F32),
                       _Out(w_in[:, 2 * CONV_CH:].astype(BF16), BF16, scale=XA_SCALE, heads=(XA_HEADS, HEAD_DIM))],
                  tm=tm, seq=s)
    mix = _conv(a.reshape(b, s, 2 * CONV_CH), b_in, dw_w, dw_b, ln_g, ln_b, ts=ts)
    return mix.reshape(b * s, CONV_CH), xq


def kernel(x, mem, positions, nsa_w_in, nsa_cmp_pe, nsa_cmp_w1, nsa_cmp_w2, mla_w_in, mla_q_norm, mla_w_uq, mla_kv_norm, mla_w_ukv, conv_w_in, conv_b_in, conv_dw_w, conv_dw_b, conv_ln_g, conv_ln_b, mem_w_kv, w_out, ln_g, ln_b, moe_w_grp, moe_b_grp, moe_w_exp, moe_b_exp, moe_w_gate, moe_w_up, moe_w_down):
    b, s, d = x.shape
    depth = w_out.shape[0]
    n_mem = mem.shape[1]
    alpha = (2.0 * depth) ** 0.25
    tm = 512
    tq = 512
    tabs_n = _rope_tables(positions, ROT_DIM, ROPE_THETA, HEAD_DIM)
    tabs_m = _rope_tables(positions, MLA_ROPE, MLA_THETA, MLA_ROPE)
    tabs_mq = _rope_tables(positions, MLA_ROPE, MLA_THETA, LANES, offset=MLA_NOPE)
    x2 = x.reshape(b * s, d)
    mem2 = mem.reshape(b * n_mem, d)
    for i in range(depth):
        kind, j = i % N_MIXERS, i // N_MIXERS
        if kind == 0:
            mix, xq = _nsa_mixer(x2, b, s, nsa_w_in[j], nsa_cmp_pe[j], nsa_cmp_w1[j], nsa_cmp_w2[j],
                                 tabs_n, tm=tm, tq=tq)
        elif kind == 1:
            mix, xq = _mla_mixer(x2, b, s, mla_w_in[j], mla_q_norm[j], mla_w_uq[j], mla_kv_norm[j],
                                 mla_w_ukv[j], tabs_m, tabs_mq, tm=tm, tq=tq)
        else:
            mix, xq = _conv_mixer(x2, b, s, conv_w_in[j], conv_b_in[j], conv_dw_w[j], conv_dw_b[j],
                                  conv_ln_g[j], conv_ln_b[j], tm=tm, ts=tq)
        xah = (XA_HEADS, HEAD_DIM)
        mk, mv1 = _proj(mem2, [_Out(mem_w_kv[i][:, :XA_WIDTH].astype(BF16), BF16, heads=xah),
                               _Out(mem_w_kv[i][:, XA_WIDTH:].astype(BF16), BF16, heads=xah,
                                    pad=("ones", HEAD_DIM))], tm=tm, seq=n_mem)
        xa = _flash(xq.reshape(b * XA_HEADS, 1, s, HEAD_DIM), mk.reshape(b * XA_HEADS, n_mem, HEAD_DIM),
                    mv1.reshape(b * XA_HEADS, n_mem, 2 * HEAD_DIM), mode="full", tq=min(FLASH_TQ, s), tk=n_mem,
                    out_dtype=BF16).reshape(b, XA_HEADS, s, HEAD_DIM)
        w_route = jnp.zeros((d, LANES), F32).at[:, :N_GROUPS].set(moe_w_grp[i])
        w_route = w_route.at[:, ROUTE_OFF:ROUTE_OFF + N_EXPERTS].set(moe_w_exp[i])
        b_route = jnp.zeros((1, LANES), F32).at[0, :N_GROUPS].set(moe_b_grp[i])
        b_route = b_route.at[0, ROUTE_OFF:ROUTE_OFF + N_EXPERTS].set(moe_b_exp[i])
        x1, gate = _out_proj(mix, xa, x2, w_out[i], ln_g[i, 0], ln_b[i, 0], w_route, b_route,
                             tm=tm, seq=s, alpha=alpha)
        x2 = _moe(x1, gate, moe_w_gate[i], moe_w_up[i], moe_w_down[i], ln_g[i, 1], ln_b[i, 1], alpha=alpha)
    return x2.reshape(b, s, d)
```

```python
import functools
import math

import numpy as np
import jax
import jax.numpy as jnp
from jax import lax
from jax.experimental import pallas as pl
from jax.experimental.pallas import tpu as pltpu

F32 = jnp.float32
BF16 = jnp.bfloat16

D_MODEL = 1024
N_MIXERS = 3
HEAD_DIM = 64
MIX_HEADS = 12
MIX_WIDTH = MIX_HEADS * HEAD_DIM
XA_HEADS = 4
XA_WIDTH = XA_HEADS * HEAD_DIM
ROPE_THETA = 500000.0
ROT_DIM = HEAD_DIM // 4
LN_EPS = 1e-5
RMS_EPS = 1e-6
NSA_KV_HEADS = 4
NSA_GROUP = MIX_HEADS // NSA_KV_HEADS
NSA_KV_WIDTH = NSA_KV_HEADS * HEAD_DIM
CMP_LEN = 32
CMP_STRIDE = 16
CMP_HIDDEN = 128
SEL_BLOCK = 64
SEL_SHIFT = 6
SEL_TOPK = 16
WINDOW = 512
MLA_Q_RANK = 256
MLA_KV_RANK = 128
MLA_NOPE = 64
MLA_ROPE = 32
MLA_V = 64
MLA_THETA = 10000.0
CONV_CH = MIX_WIDTH
CONV_WIDTH = 31
N_GROUPS = 4
EXPERTS_PER_GROUP = 4
N_EXPERTS = N_GROUPS * EXPERTS_PER_GROUP
EXPERT_FF = 512

LANES = 128
SUBLANES = 8
MXU_N = 256
VMEM_LIMIT = 56 * 1024 * 1024
NEG = -1e30
ROUTE_OFF = N_GROUPS
LOG2E = math.log2(math.e)
XA_SCALE = LOG2E / math.sqrt(HEAD_DIM)
FLASH_TQ = 1024
FLASH_DB = 512


def _cparams(*sem):
    return pltpu.CompilerParams(dimension_semantics=sem, vmem_limit_bytes=VMEM_LIMIT)


def _dot(a, b):
    return jnp.dot(a, b, preferred_element_type=F32)


def _dot_nt(a, b):
    return lax.dot_general(a, b, (((1,), (1,)), ((), ())), preferred_element_type=F32)


def _layer_norm(z, g, b):
    mu = jnp.mean(z, -1, keepdims=True)
    zc = z - mu
    var = jnp.mean(zc * zc, -1, keepdims=True)
    return zc * lax.rsqrt(var + LN_EPS) * g + b


def _sigmoid(x):
    return 1.0 / (1.0 + jnp.exp(-x))


def _rope(acc, r, c_ref, s1_ref, s2_ref):
    n = acc.shape[1]
    rep = n // LANES
    c = jnp.tile(c_ref[...], (1, rep))
    s1 = jnp.tile(s1_ref[...], (1, rep))
    s2 = jnp.tile(s2_ref[...], (1, rep))
    return acc * c + pltpu.roll(acc, n - r, 1) * s1 + pltpu.roll(acc, r, 1) * s2


class _Out:
    def __init__(self, w, dtype, rope=False, scale=1.0, heads=None, pad=None):
        self.w, self.dtype, self.rope, self.scale, self.heads, self.pad = w, dtype, rope, scale, heads, pad


def _proj_kernel(*refs, outs, rms, rope_r, has_extra, tm, spt):
    it = iter(refs)
    x_ref = next(it)
    g_ref = next(it) if rms else None
    tabs = (next(it), next(it), next(it)) if rope_r else None
    extra_ref = next(it) if has_extra else None
    w_refs = [next(it) for _ in outs]
    o_refs = [next(it) for _ in outs]
    xv = x_ref[...]
    if rms:
        xf = xv.astype(F32)
        xf = xf * lax.rsqrt(jnp.mean(xf * xf, -1, keepdims=True) + RMS_EPS) * g_ref[...]
        xb = xf.astype(BF16)
    else:
        xb = xv.astype(BF16)

    def pad_tile(kind, width):
        if kind == "onehot":
            tok = (pl.program_id(0) % spt) * tm + lax.broadcasted_iota(jnp.int32, (tm, width), 0)
            blk = lax.broadcasted_iota(jnp.int32, (tm, width), 1)
            return jnp.where(jnp.right_shift(tok, SEL_SHIFT) == blk, 1.0, 0.0)
        if kind == "extra":
            return extra_ref[:, :width].astype(F32)
        return jnp.full((tm, width), 1.0 if kind == "ones" else 0.0, F32)

    for o, w_ref, o_ref in zip(outs, w_refs, o_refs):
        n = w_ref.shape[1]
        step = MXU_N if n % MXU_N == 0 else LANES
        pad = pad_tile(*o.pad) if o.pad else None
        for c0 in range(0, n, step):
            acc = _dot(xb, w_ref[:, c0:c0 + step])
            if o.rope:
                acc = _rope(acc, rope_r, *tabs)
            if o.scale != 1.0:
                acc = acc * o.scale
            if o.heads is None:
                o_ref[:, c0:c0 + step] = acc.astype(o_ref.dtype)
            else:
                hw = o.heads[1]
                for j in range(step // hw):
                    piece = acc[:, j * hw:(j + 1) * hw]
                    if pad is not None:
                        piece = jnp.concatenate([piece, pad], axis=1)
                    o_ref[0, c0 // hw + j] = piece.astype(o_ref.dtype)


def _proj(x, outs, *, tm, seq, x_cols=None, rms_gain=None, rope=None, extra=None):
    m = x.shape[0]
    tm = min(tm, seq)
    spt = seq // tm
    kdim, xj = x_cols if x_cols else (x.shape[1], 0)
    ins = [x]
    in_specs = [pl.BlockSpec((tm, kdim), lambda i: (i, xj))]
    if rms_gain is not None:
        ins.append(rms_gain.reshape(1, kdim).astype(F32))
        in_specs.append(pl.BlockSpec((1, kdim), lambda i: (0, 0)))
    rope_r = 0
    if rope is not None:
        rope_r = rope[0]
        for t in rope[1:]:
            ins.append(t)
            in_specs.append(pl.BlockSpec((tm, LANES), lambda i: (i, 0)))
    if extra is not None:
        ins.append(extra)
        in_specs.append(pl.BlockSpec((tm, LANES), lambda i: (i, 0)))
    for o in outs:
        ins.append(o.w)
        in_specs.append(pl.BlockSpec(o.w.shape, lambda i: (0, 0)))
    out_shape, out_specs = [], []
    for o in outs:
        if o.heads is None:
            out_shape.append(jax.ShapeDtypeStruct((m, o.w.shape[1]), o.dtype))
            out_specs.append(pl.BlockSpec((tm, o.w.shape[1]), lambda i: (i, 0)))
        else:
            nh, hw = o.heads
            assert nh * hw == o.w.shape[1]
            width = hw + (o.pad[1] if o.pad else 0)
            out_shape.append(jax.ShapeDtypeStruct((m // seq, nh, seq, width), o.dtype))
            out_specs.append(pl.BlockSpec((1, nh, tm, width), lambda i: (i // spt, 0, i % spt, 0)))
    kern = functools.partial(_proj_kernel, outs=tuple(outs), rms=rms_gain is not None, rope_r=rope_r,
                             has_extra=extra is not None, tm=tm, spt=spt)
    return pl.pallas_call(kern, grid=(m // tm,), in_specs=in_specs, out_specs=out_specs,
                          out_shape=out_shape, compiler_params=_cparams("parallel"))(*ins)


def _flash_kernel(q_ref, k_ref, v_ref, o_ref, m_ref, acc_ref, *, mode, g, tq, tk, db):
    i = pl.program_id(1)
    dv = o_ref.shape[-1]
    m_ref[...] = jnp.full(m_ref.shape, NEG, F32)
    acc_ref[...] = jnp.zeros(acc_ref.shape, F32)

    def block(h, r0, nr, kstart, nk, kind):
        kt = k_ref[0, pl.ds(kstart, nk), :]
        vt = v_ref[0, pl.ds(kstart, nk), :]
        s = _dot_nt(q_ref[0, h, r0:r0 + nr, :], kt)
        if kind is not None:
            row = lax.broadcasted_iota(jnp.int32, (nr, nk), 0)
            col = lax.broadcasted_iota(jnp.int32, (nr, nk), 1)
            s = jnp.where(col <= row if kind == "diag" else col > row, s, NEG)
        a0 = h * tq + r0
        m_prev = m_ref[a0:a0 + nr, :]
        m_new = jnp.maximum(m_prev, jnp.max(s, -1, keepdims=True))
        alpha = jnp.exp2(m_prev - m_new)
        p = jnp.exp2(s - jnp.tile(m_new, (1, nk // LANES)))
        acc_ref[a0:a0 + nr, :] = alpha * acc_ref[a0:a0 + nr, :] + _dot(p.astype(BF16), vt)
        m_ref[a0:a0 + nr, :] = m_new

    if mode == "full":
        for j in range(k_ref.shape[1] // tk):
            for h in range(g):
                block(h, 0, tq, j * tk, tk, None)
    elif mode == "window":
        t0 = pl.multiple_of(i * tq, tq)
        for h in range(g):
            block(h, 0, tq, t0, tq, "diag")

        @pl.when(i > 0)
        def _():
            for h in range(g):
                block(h, 0, tq, pl.multiple_of(t0 - tq, tq), tq, "prev")
    else:
        t0 = pl.multiple_of(i * tq, tq)
        for a in range(tq // db):
            for h in range(g):
                block(h, a * db, db, pl.multiple_of(t0 + a * db, db), db, "diag")
                for c in range(a):
                    block(h, a * db, db, pl.multiple_of(t0 + c * db, db), db, None)

        def body(j, carry):
            for h in range(g):
                block(h, 0, tq, pl.multiple_of(j * tk, tk), tk, None)
            return carry
        lax.fori_loop(0, i * (tq // tk), body, 0)

    acc = acc_ref[...]
    o = acc / pltpu.roll(acc, dv, 1)
    o_ref[0] = o[:, :dv].reshape(g, tq, dv).astype(o_ref.dtype)


def _flash(q, k, v_ones, *, mode, tq, tk, out_dtype, db=None):
    n, g, s, dk = q.shape
    sk, dv = v_ones.shape[1], v_ones.shape[2] // 2
    assert 2 * dv == LANES
    kern = functools.partial(_flash_kernel, mode=mode, g=g, tq=tq, tk=tk, db=db or tq)
    return pl.pallas_call(
        kern, grid=(n, s // tq),
        in_specs=[pl.BlockSpec((1, g, tq, dk), lambda b, i: (b, 0, i, 0)),
                  pl.BlockSpec((1, sk, dk), lambda b, i: (b, 0, 0)),
                  pl.BlockSpec((1, sk, 2 * dv), lambda b, i: (b, 0, 0))],
        out_specs=pl.BlockSpec((1, g, tq, dv), lambda b, i: (b, 0, i, 0)),
        out_shape=jax.ShapeDtypeStruct((n, g, s, dv), out_dtype),
        scratch_shapes=[pltpu.VMEM((g * tq, LANES), F32), pltpu.VMEM((g * tq, 2 * dv), F32)],
        compiler_params=_cparams("parallel", "arbitrary"))(q, k, v_ones)


def _cmp_kernel(x_ref, w1s_ref, pe_ref, w1_ref, w2_ref, c_ref, s1_ref, s2_ref, o_ref, *, rope):
    nch = x_ref.shape[1]
    ab = _dot(x_ref[0], w1s_ref[...])
    pe_term = _dot(pe_ref[...], w1_ref[...])[0:1, :]
    nxt = pltpu.roll(ab[:, CMP_HIDDEN:], nch - 1, 0)
    h = ab[:, :CMP_HIDDEN] + nxt + pe_term
    h = h * _sigmoid(h)
    o = _dot(h.astype(BF16), w2_ref[...])
    if rope:
        o = _rope(o, ROT_DIM // 2, c_ref.at[0], s1_ref.at[0], s2_ref.at[0])
    o_ref[0] = o[:, :HEAD_DIM].astype(o_ref.dtype)


def _compress(tc, pe, w1, w2, tabs, *, rope, kvh):
    n, nch, width = tc.shape
    half = CMP_STRIDE * HEAD_DIM
    w1b = w1.astype(BF16)
    w1s = jnp.concatenate([w1b[:half], w1b[half:]], axis=1)
    pe8 = jnp.zeros((8, CMP_LEN * HEAD_DIM), BF16).at[0].set(pe.reshape(-1).astype(BF16))
    w2p = jnp.zeros((CMP_HIDDEN, LANES), BF16).at[:, :HEAD_DIM].set(w2.astype(BF16))
    kern = functools.partial(_cmp_kernel, rope=rope)
    tab_spec = pl.BlockSpec((1, nch, LANES), lambda i: (i // kvh, 0, 0))
    return pl.pallas_call(
        kern, grid=(n,),
        in_specs=[pl.BlockSpec((1, nch, width), lambda i: (i, 0, 0)),
                  pl.BlockSpec(w1s.shape, lambda i: (0, 0)),
                  pl.BlockSpec(pe8.shape, lambda i: (0, 0)),
                  pl.BlockSpec(w1b.shape, lambda i: (0, 0)),
                  pl.BlockSpec(w2p.shape, lambda i: (0, 0)),
                  tab_spec, tab_spec, tab_spec],
        out_specs=pl.BlockSpec((1, nch, HEAD_DIM), lambda i: (i, 0, 0)),
        out_shape=jax.ShapeDtypeStruct((n, nch, HEAD_DIM), BF16),
        compiler_params=_cparams("parallel"))(tc, w1s, pe8, w1b, w2p, *tabs)


def _cmp_attn_kernel(q_ref, kc_ref, vc_ref, ovl_ref, oc_ref, qa_ref, *, g, tq, n_cmp, scale):
    i = pl.program_id(1)
    nch = kc_ref.shape[1]
    n_sel = ovl_ref.shape[0]
    kc = kc_ref[0]
    vc = vc_ref[0]
    t0 = i * tq
    t_lane = t0 + lax.broadcasted_iota(jnp.int32, (nch, tq), 1)
    n_sub = lax.broadcasted_iota(jnp.int32, (nch, tq), 0)
    vis_t = (n_sub < n_cmp) & (n_sub * CMP_STRIDE + (CMP_LEN - 1) <= t_lane)
    p_sum_t = jnp.zeros((nch, tq), F32)
    for h in range(g):
        qh = q_ref[0, h]
        st = jnp.where(vis_t, _dot_nt(kc, qh) * scale, NEG)
        mt = jnp.max(st, 0, keepdims=True)
        et = jnp.where(vis_t, jnp.exp(st - mt), 0.0)
        pt = et / jnp.maximum(jnp.sum(et, 0, keepdims=True), 1e-30)
        p_sum_t = p_sum_t + pt
        oc_ref[0, h] = lax.dot_general(pt.astype(BF16), vc, (((0,), (0,)), ((), ())),
                                       preferred_element_type=F32).astype(oc_ref.dtype)
    imp = jnp.dot(ovl_ref[...], p_sum_t, preferred_element_type=F32,
                  precision=lax.Precision.HIGHEST)
    t1 = t0 + lax.broadcasted_iota(jnp.int32, (n_sel, tq), 1)
    blk = lax.broadcasted_iota(jnp.int32, (n_sel, tq), 0)
    cur = jnp.right_shift(t1, SEL_SHIFT)
    valid = blk <= cur
    forced = (blk == 0) | (blk == cur) | (blk == cur - 1)
    imp = jnp.where(valid, jnp.where(forced, jnp.inf, imp), -jnp.inf)
    nv = n_sel // SUBLANES
    groups = [imp[v * SUBLANES:(v + 1) * SUBLANES] for v in range(nv)]
    sub = lax.broadcasted_iota(jnp.int32, (SUBLANES, tq), 0)

    def count_group(rb, ranks):
        ranks = list(ranks)
        for rr in range(SUBLANES):
            row = groups[rb][rr:rr + 1, :]
            for v in range(nv):
                if v > rb:
                    ahead = jnp.where(row >= groups[v], 1.0, 0.0)
                elif v < rb:
                    ahead = jnp.where(row > groups[v], 1.0, 0.0)
                else:
                    ahead = jnp.where(sub > rr, jnp.where(row >= groups[v], 1.0, 0.0),
                                      jnp.where(row > groups[v], 1.0, 0.0))
                ranks[v] = ranks[v] + ahead
        return tuple(ranks)

    last_group = jnp.right_shift(t0 + (tq - 1), SEL_SHIFT) // SUBLANES
    ranks = tuple(jnp.zeros((SUBLANES, tq), F32) for _ in range(nv))
    for rb in range(nv):
        ranks = lax.cond(rb <= last_group, functools.partial(count_group, rb), lambda rk: rk, ranks)
    rank = jnp.concatenate(ranks, axis=0)
    chosen_t = jnp.where((rank < float(min(SEL_TOPK, n_sel))) & valid, 1.0, 0.0).astype(BF16)
    eye = (lax.broadcasted_iota(jnp.int32, (tq, tq), 0)
           == lax.broadcasted_iota(jnp.int32, (tq, tq), 1))
    chosen = _dot_nt(jnp.where(eye, 1.0, 0.0).astype(BF16), chosen_t)
    bias = (chosen - 1.0) * (-NEG)
    for h in range(g):
        qs = q_ref[0, h].astype(F32) * (scale * LOG2E)
        qa_ref[0, h] = jnp.concatenate([qs, bias], axis=1).astype(qa_ref.dtype)


def _cmp_attn(q, kc, vc, ovl_t, *, tq, n_cmp):
    n, g, s, d = q.shape
    nch = kc.shape[1]
    n_sel = ovl_t.shape[0]
    kern = functools.partial(_cmp_attn_kernel, g=g, tq=tq, n_cmp=n_cmp, scale=1.0 / math.sqrt(HEAD_DIM))
    return pl.pallas_call(
        kern, grid=(n, s // tq),
        in_specs=[pl.BlockSpec((1, g, tq, d), lambda b, i: (b, 0, i, 0)),
                  pl.BlockSpec((1, nch, d), lambda b, i: (b, 0, 0)),
                  pl.BlockSpec((1, nch, d), lambda b, i: (b, 0, 0)),
                  pl.BlockSpec(ovl_t.shape, lambda b, i: (0, 0))],
        out_specs=[pl.BlockSpec((1, g, tq, d), lambda b, i: (b, 0, i, 0)),
                   pl.BlockSpec((1, g, tq, d + n_sel), lambda b, i: (b, 0, i, 0))],
        out_shape=[jax.ShapeDtypeStruct((n, g, s, d), F32),
                   jax.ShapeDtypeStruct((n, g, s, d + n_sel), BF16)],
        compiler_params=_cparams("parallel", "arbitrary"))(q, kc, vc, ovl_t)


def _combine_kernel(gl_ref, oc_ref, os_ref, ow_ref, o_ref, *, nh):
    gates = _sigmoid(gl_ref[...])
    for h in range(nh):
        o = (gates[:, 3 * h:3 * h + 1] * oc_ref[0, h] + gates[:, 3 * h + 1:3 * h + 2] * os_ref[0, h]
             + gates[:, 3 * h + 2:3 * h + 3] * ow_ref[0, h])
        o_ref[0, h] = o.astype(o_ref.dtype)


def _combine(gl, oc, osel, ow, *, tq):
    bsz, nh, s, d = oc.shape
    spt = s // tq
    spec = pl.BlockSpec((1, nh, tq, d), lambda b, i: (b, 0, i, 0))
    return pl.pallas_call(
        functools.partial(_combine_kernel, nh=nh), grid=(bsz, spt),
        in_specs=[pl.BlockSpec((tq, LANES), lambda b, i: (b * spt + i, 0)), spec, spec, spec],
        out_specs=spec, out_shape=jax.ShapeDtypeStruct((bsz, nh, s, d), BF16),
        compiler_params=_cparams("parallel", "parallel"))(gl, oc, osel, ow)


CONV_HALO = 32
CONV_ROWS = 64


def _conv_kernel(cur_ref, prev_ref, bin_ref, w_ref, wb_ref, g_ref, b_ref, o_ref, u_ref, al_ref, *, ts):
    i = pl.program_id(1)
    ch = o_ref.shape[-1]

    def glu(a):
        a = a + bin_ref[...]
        return a[:, :ch] * _sigmoid(a[:, ch:])

    u_ref[CONV_HALO:CONV_HALO + ts, :] = glu(cur_ref[0])

    @pl.when(i == 0)
    def _():
        u_ref[0:CONV_HALO, :] = jnp.zeros((CONV_HALO, ch), F32)

    @pl.when(i > 0)
    def _():
        u_ref[0:CONV_HALO, :] = glu(prev_ref[0])

    base = CONV_HALO - (CONV_WIDTH - 1)
    for r0 in range(0, ts, CONV_ROWS):
        acc = jnp.zeros((CONV_ROWS, ch), F32)
        for sft in range(SUBLANES):
            taps = [k for k in range(CONV_WIDTH) if (base + k) % SUBLANES == sft]
            lo = base + taps[0]
            span = taps[-1] - taps[0] + CONV_ROWS
            al_ref[0:span, :] = u_ref[r0 + lo:r0 + lo + span, :]
            for k in taps:
                off = base + k - lo
                acc = acc + al_ref[off:off + CONV_ROWS, :] * w_ref[k:k + 1, :]
        y = _layer_norm(acc + wb_ref[...], g_ref[...], b_ref[...])
        o_ref[0, r0:r0 + CONV_ROWS, :] = (y * _sigmoid(y)).astype(o_ref.dtype)


def _conv(a, b_in, dw_w, dw_b, ln_g, ln_b, *, ts):
    bsz, s, two_ch = a.shape
    ch = two_ch // 2
    hb = ts // CONV_HALO
    row = lambda v: v.reshape(1, -1).astype(F32)
    return pl.pallas_call(
        functools.partial(_conv_kernel, ts=ts), grid=(bsz, s // ts),
        in_specs=[pl.BlockSpec((1, ts, two_ch), lambda b, i: (b, i, 0)),
                  pl.BlockSpec((1, CONV_HALO, two_ch), lambda b, i: (b, jnp.maximum(i * hb - 1, 0), 0)),
                  pl.BlockSpec((1, two_ch), lambda b, i: (0, 0)),
                  pl.BlockSpec((CONV_WIDTH, ch), lambda b, i: (0, 0)),
                  pl.BlockSpec((1, ch), lambda b, i: (0, 0)),
                  pl.BlockSpec((1, ch), lambda b, i: (0, 0)),
                  pl.BlockSpec((1, ch), lambda b, i: (0, 0))],
        out_specs=pl.BlockSpec((1, ts, ch), lambda b, i: (b, i, 0)),
        out_shape=jax.ShapeDtypeStruct((bsz, s, ch), BF16),
        scratch_shapes=[pltpu.VMEM((CONV_HALO + ts, ch), F32), pltpu.VMEM((CONV_HALO + CONV_ROWS, ch), F32)],
        compiler_params=_cparams("parallel", "arbitrary"))(
            a, a, row(b_in), dw_w.astype(F32), row(dw_b), row(ln_g), row(ln_b))


def _out_kernel(mix_ref, xa_ref, x_ref, w_ref, g_ref, b_ref, wr_ref, br_ref, x1_ref, gate_ref,
                cat_ref, *, alpha, mix_heads):
    if mix_heads:
        for h in range(MIX_HEADS):
            cat_ref[:, h * HEAD_DIM:(h + 1) * HEAD_DIM] = mix_ref[0, h].astype(F32)
    else:
        cat_ref[:, :MIX_WIDTH] = mix_ref[...].astype(F32)
    for h in range(XA_HEADS):
        cat_ref[:, MIX_WIDTH + h * HEAD_DIM:MIX_WIDTH + (h + 1) * HEAD_DIM] = xa_ref[0, h].astype(F32)
    y = _dot(cat_ref[...].astype(BF16), w_ref[...])
    x1 = _layer_norm(alpha * x_ref[...] + y, g_ref[...], b_ref[...])
    x1_ref[...] = x1
    hi = x1.astype(BF16)
    lo = (x1 - hi.astype(F32)).astype(BF16)
    wr = wr_ref[...]
    w_hi = wr.astype(BF16)
    w_lo = (wr - w_hi.astype(F32)).astype(BF16)
    logits = ((_dot(hi, w_hi) + _dot(lo, w_hi)) + (_dot(hi, w_lo) + _dot(lo, w_lo)) + br_ref[...])
    lane = lax.broadcasted_iota(jnp.int32, logits.shape, 1)
    far = jnp.int32(LANES)
    lg = jnp.where(lane < N_GROUPS, logits, -jnp.inf)
    mg = jnp.max(lg, -1, keepdims=True)
    g_prob = 1.0 / jnp.sum(jnp.exp(lg - mg), -1, keepdims=True)
    g_sel = jnp.min(jnp.where(lg == mg, lane, far), -1, keepdims=True)
    lo = ROUTE_OFF + EXPERTS_PER_GROUP * g_sel
    le = jnp.where((lane >= lo) & (lane < lo + EXPERTS_PER_GROUP), logits, -jnp.inf)
    m1 = jnp.max(le, -1, keepdims=True)
    i1 = jnp.min(jnp.where(le == m1, lane, far), -1, keepdims=True)
    le2 = jnp.where(lane == i1, -jnp.inf, le)
    m2 = jnp.max(le2, -1, keepdims=True)
    i2 = jnp.min(jnp.where(le2 == m2, lane, far), -1, keepdims=True)
    e2 = jnp.exp(m2 - m1)
    p1 = g_prob / (1.0 + e2)
    gates = jnp.where(lane == i1, p1, jnp.where(lane == i2, p1 * e2, 0.0))
    gate_ref[...] = jnp.where(lane == 0, g_sel.astype(F32), gates)


def _out_proj(mix, xa, x, w_out, ln_g, ln_b, w_route, b_route, *, tm, seq, alpha):
    t, d = x.shape
    tm = min(tm, seq)
    spt = seq // tm
    mix_heads = mix.ndim == 4
    row = lambda v: v.reshape(1, -1).astype(F32)
    full = lambda a: pl.BlockSpec(a.shape, lambda i: (0, 0))
    heads = lambda nh: pl.BlockSpec((1, nh, tm, HEAD_DIM), lambda i: (i // spt, 0, i % spt, 0))
    ins = [mix, xa, x, w_out.astype(BF16), row(ln_g), row(ln_b), w_route, b_route]
    in_specs = [heads(MIX_HEADS) if mix_heads else pl.BlockSpec((tm, MIX_WIDTH), lambda i: (i, 0)),
                heads(XA_HEADS),
                pl.BlockSpec((tm, d), lambda i: (i, 0))] + [full(a) for a in ins[3:]]
    return pl.pallas_call(
        functools.partial(_out_kernel, alpha=alpha, mix_heads=mix_heads), grid=(t // tm,), in_specs=in_specs,
        out_specs=[pl.BlockSpec((tm, d), lambda i: (i, 0)), pl.BlockSpec((tm, LANES), lambda i: (i, 0))],
        out_shape=[jax.ShapeDtypeStruct((t, d), F32), jax.ShapeDtypeStruct((t, LANES), F32)],
        scratch_shapes=[pltpu.VMEM((tm, MIX_WIDTH + XA_WIDTH), F32)],
        compiler_params=_cparams("parallel"))(*ins)


MOE_CHUNK = 2048
MOE_TILE = 256
MOE_UNROLL = 8
MOE_ALIGN = 16


def _moe_kernel(order_ref, meta_ref, x_ref, gate_ref, wg_ref, wu_ref, wd_ref, g_ref, b_ref, o_ref,
                xs_ref, gs_ref, ys_ref, tmp_ref, *, alpha, ch, tile):
    e = pl.program_id(1)
    grp = e // EXPERTS_PER_GROUP
    k = e % EXPERTS_PER_GROUP
    start = meta_ref[0, 0, grp]
    n_rows = meta_ref[0, 0, N_GROUPS + grp]
    n_copy = (n_rows + (MOE_UNROLL - 1)) // MOE_UNROLL * MOE_UNROLL
    n_mm = (n_rows + (MOE_ALIGN - 1)) // MOE_ALIGN * MOE_ALIGN
    n_full = n_mm // tile
    tail_start = pl.multiple_of(jnp.maximum(n_mm - tile, 0), MOE_ALIGN)
    done_rows = n_full * tile

    def token(pos):
        return order_ref[0, 0, start + pos]

    @pl.when((pl.program_id(0) == 0) & (e == 0))
    def _():
        tmp_ref[...] = jnp.zeros(tmp_ref.shape, F32)
        gs_ref[...] = jnp.zeros(gs_ref.shape, F32)
        ys_ref[...] = jnp.zeros(ys_ref.shape, F32)

    @pl.when(k == 0)
    def _():
        def gather_tile(rt, carry):
            r0 = pl.multiple_of(rt * tile, tile)

            def rows(blk, c2):
                i0 = pl.multiple_of(blk * MOE_UNROLL, MOE_UNROLL)
                g0 = pl.multiple_of(r0 + i0, MOE_UNROLL)
                for u in range(MOE_UNROLL):
                    t = token(g0 + u)
                    tmp_ref[pl.ds(i0 + u, 1), :] = x_ref[pl.ds(t, 1), :]
                    gs_ref[pl.ds(g0 + u, 1), :] = gate_ref[pl.ds(t, 1), :]
                return c2
            lax.fori_loop(0, jnp.minimum(n_copy - r0, tile) // MOE_UNROLL, rows, 0)
            xs_ref[pl.ds(r0, tile), :] = tmp_ref[...].astype(BF16)
            return carry
        lax.fori_loop(0, (n_copy + (tile - 1)) // tile, gather_tile, 0)

    lane = lax.broadcasted_iota(jnp.int32, (tile, LANES), 1)

    def expert_rows(r0, tail):
        xb = xs_ref[pl.ds(r0, tile), :]
        gcol = jnp.sum(jnp.where(lane == ROUTE_OFF + e, gs_ref[pl.ds(r0, tile), :], 0.0), -1, keepdims=True)
        if tail:
            fresh = r0 + lax.broadcasted_iota(jnp.int32, (tile, 1), 0) >= done_rows
            gcol = jnp.where(fresh, gcol, 0.0)
        hg = _dot(xb, wg_ref[0])
        h = hg * _sigmoid(hg) * _dot(xb, wu_ref[0]) * gcol
        y = _dot(h.astype(BF16), wd_ref[0])

        @pl.when(k == 0)
        def _():
            if tail:
                ys_ref[pl.ds(r0, tile), :] = jnp.where(fresh, y, ys_ref[pl.ds(r0, tile), :])
            else:
                ys_ref[pl.ds(r0, tile), :] = y

        @pl.when(k > 0)
        def _():
            ys_ref[pl.ds(r0, tile), :] += y

    def full_tile(rt, carry):
        expert_rows(pl.multiple_of(rt * tile, tile), False)
        return carry
    lax.fori_loop(0, n_full, full_tile, 0)

    @pl.when(n_mm > done_rows)
    def _():
        expert_rows(tail_start, True)

    @pl.when(k == EXPERTS_PER_GROUP - 1)
    def _():
        def rows(blk, c2):
            i0 = pl.multiple_of(blk * MOE_UNROLL, MOE_UNROLL)
            for u in range(MOE_UNROLL):
                o_ref[pl.ds(token(i0 + u), 1), :] = ys_ref[pl.ds(i0 + u, 1), :]
            return c2
        lax.fori_loop(0, n_copy // MOE_UNROLL, rows, 0)

    @pl.when(e == pl.num_programs(1) - 1)
    def _():
        for r0 in range(0, ch, tile):
            z = alpha * x_ref[r0:r0 + tile, :] + o_ref[r0:r0 + tile, :]
            o_ref[r0:r0 + tile, :] = _layer_norm(z, g_ref[...], b_ref[...])


def _moe(x1, gate, wg, wu, wd, ln_g, ln_b, *, alpha):
    t, d = x1.shape
    ne, _, ff = wg.shape
    ch = min(MOE_CHUNK, t)
    nc = t // ch
    gsel = gate[:, 0].astype(jnp.int32).reshape(nc, ch)
    order = jnp.argsort(gsel, axis=1, stable=True).astype(jnp.int32)
    order = jnp.concatenate([order, jnp.broadcast_to(order[:, -1:], (nc, MOE_TILE))], axis=1)
    counts = jnp.sum(gsel[:, :, None] == jnp.arange(N_GROUPS, dtype=jnp.int32), axis=1, dtype=jnp.int32)
    starts = jnp.cumsum(counts, axis=1, dtype=jnp.int32) - counts
    meta = jnp.concatenate([starts, counts], axis=1)
    row = lambda v: v.reshape(1, -1).astype(F32)
    smem = lambda n: pl.BlockSpec((1, 1, n), lambda c, e: (c, 0, 0), memory_space=pltpu.SMEM)
    return pl.pallas_call(
        functools.partial(_moe_kernel, alpha=alpha, ch=ch, tile=MOE_TILE), grid=(nc, ne),
        in_specs=[smem(ch + MOE_TILE), smem(2 * N_GROUPS),
                  pl.BlockSpec((ch, d), lambda c, e: (c, 0), pipeline_mode=pl.Buffered(1)),
                  pl.BlockSpec((ch, LANES), lambda c, e: (c, 0)),
                  pl.BlockSpec((1, d, ff), lambda c, e: (e, 0, 0)),
                  pl.BlockSpec((1, d, ff), lambda c, e: (e, 0, 0)),
                  pl.BlockSpec((1, ff, d), lambda c, e: (e, 0, 0)),
                  pl.BlockSpec((1, d), lambda c, e: (0, 0)),
                  pl.BlockSpec((1, d), lambda c, e: (0, 0))],
        out_specs=pl.BlockSpec((ch, d), lambda c, e: (c, 0)),
        out_shape=jax.ShapeDtypeStruct((t, d), F32),
        scratch_shapes=[pltpu.VMEM((ch, d), BF16), pltpu.VMEM((ch, LANES), F32), pltpu.VMEM((ch, d), F32),
                        pltpu.VMEM((MOE_TILE, d), F32)],
        compiler_params=_cparams("parallel", "arbitrary"))(
            order.reshape(nc, 1, ch + MOE_TILE), meta.reshape(nc, 1, 2 * N_GROUPS), x1, gate,
            wg.astype(BF16), wu.astype(BF16), wd.astype(BF16), row(ln_g), row(ln_b))


def _rope_tables(positions, dim, theta, period, offset=0):
    b, s = positions.shape
    half = dim // 2
    inv = theta ** (-jnp.arange(0, dim, 2, dtype=F32) / dim)
    ang = positions.astype(F32)[..., None] * inv
    cos, sin = jnp.cos(ang), jnp.sin(ang)
    rest = period - dim - offset
    fill = lambda v, n: jnp.full((b, s, n), v, F32)
    c = jnp.concatenate([fill(1.0, offset), cos, cos, fill(1.0, rest)], -1)
    s1 = jnp.concatenate([fill(0.0, offset), -sin, fill(0.0, half + rest)], -1)
    s2 = jnp.concatenate([fill(0.0, offset + half), sin, fill(0.0, rest)], -1)
    rep = LANES // period
    return tuple(jnp.tile(t, (1, 1, rep)).reshape(b * s, LANES) for t in (c, s1, s2))


def _nsa_mixer(x2, b, s, w_in, cmp_pe, cmp_w1, cmp_w2, tabs_n, *, tm, tq):
    kv, g, d = NSA_KV_HEADS, NSA_GROUP, HEAD_DIM
    offs = np.cumsum([0, MIX_WIDTH] + [NSA_KV_WIDTH] * 6 + [3 * MIX_HEADS, XA_WIDTH]).tolist()
    col = lambda k: w_in[:, offs[k]:offs[k + 1]]
    wq, wkc, wvc, wks, wvs, wkw, wvw, wgt, wxq = [col(k) for k in range(9)]
    w_gate = jnp.zeros((D_MODEL, LANES), F32).at[:, :3 * MIX_HEADS].set(wgt).astype(BF16)
    n_sel = s // SEL_BLOCK
    bf = lambda w: w.astype(BF16)
    kvh = (kv, d)
    q, k_aug, kw_aug, kc, vc, vs1, vw1, pg, xq = _proj(
        x2, [_Out(bf(wq), BF16, rope=True, heads=(MIX_HEADS, d)),
             _Out(bf(wks), BF16, rope=True, heads=kvh, pad=("onehot", n_sel)),
             _Out(bf(wkw), BF16, rope=True, heads=kvh, pad=("zeros", n_sel)),
             _Out(bf(wkc), BF16, heads=kvh), _Out(bf(wvc), BF16, heads=kvh),
             _Out(bf(wvs), BF16, heads=kvh, pad=("ones", d)), _Out(bf(wvw), BF16, heads=kvh, pad=("ones", d)),
             _Out(w_gate, F32), _Out(bf(wxq), BF16, scale=XA_SCALE, heads=(XA_HEADS, d))],
        tm=tm, seq=s, rope=(ROT_DIM // 2,) + tuple(tabs_n))
    q = q.reshape(b * kv, g, s, d)
    nch = s // CMP_STRIDE
    kc = kc.reshape(b * kv, nch, CMP_STRIDE * d)
    vc = vc.reshape(b * kv, nch, CMP_STRIDE * d)

    n_cmp = (s - CMP_LEN) // CMP_STRIDE + 1
    end_tabs = tuple(t.reshape(b, s, LANES)[:, CMP_LEN - 1::CMP_STRIDE][:, :nch] for t in tabs_n)
    end_tabs = tuple(jnp.pad(t, ((0, 0), (0, nch - t.shape[1]), (0, 0))) for t in end_tabs)
    k_cmp = _compress(kc, cmp_pe[0], cmp_w1[0], cmp_w2[0], end_tabs, rope=True, kvh=kv)
    v_cmp = _compress(vc, cmp_pe[1], cmp_w1[1], cmp_w2[1], end_tabs, rope=False, kvh=kv)

    starts = np.arange(nch) * CMP_STRIDE
    sel_start = np.arange(n_sel) * SEL_BLOCK
    ovl = ((starts[:, None] < sel_start[None, :] + SEL_BLOCK)
           & (starts[:, None] + CMP_LEN > sel_start[None, :])
           & (np.arange(nch)[:, None] < n_cmp)).astype(np.float32)
    o_c, q_aug = _cmp_attn(q, k_cmp, v_cmp, jnp.asarray(ovl.T), tq=tq, n_cmp=n_cmp)

    ft = min(FLASH_TQ, s)
    flat = lambda a: a.reshape(b * kv, s, a.shape[-1])
    o_s = _flash(q_aug, flat(k_aug), flat(vs1), mode="causal", tq=ft, tk=ft, db=FLASH_DB, out_dtype=F32)
    o_w = _flash(q_aug, flat(kw_aug), flat(vw1), mode="window", tq=WINDOW, tk=WINDOW, out_dtype=F32)
    hm = lambda a: a.reshape(b, MIX_HEADS, s, d)
    mix = _combine(pg, hm(o_c), hm(o_s), hm(o_w), tq=tq)
    return mix, xq


def _mla_mixer(x2, b, s, w_in, q_norm, w_uq, kv_norm, w_ukv, tabs_m, tabs_mq, *, tm, tq):
    h = MIX_HEADS
    qk = MLA_NOPE + MLA_ROPE
    w_c = w_in[:, :MLA_Q_RANK + MLA_KV_RANK].astype(BF16)
    w_kr = jnp.zeros((D_MODEL, LANES), F32).at[:, :MLA_ROPE].set(
        w_in[:, MLA_Q_RANK + MLA_KV_RANK:MLA_Q_RANK + MLA_KV_RANK + MLA_ROPE]).astype(BF16)
    w_xq = w_in[:, -XA_WIDTH:].astype(BF16)
    c, kr, xq = _proj(x2, [_Out(w_c, F32), _Out(w_kr, BF16, rope=True),
                           _Out(w_xq, BF16, scale=XA_SCALE, heads=(XA_HEADS, HEAD_DIM))],
                      tm=tm, seq=s, rope=(MLA_ROPE // 2,) + tuple(tabs_m))
    wq3 = jnp.pad(w_uq.reshape(MLA_Q_RANK, h, qk), ((0, 0), (0, 0), (0, LANES - qk)))
    (q,) = _proj(c, [_Out(wq3.reshape(MLA_Q_RANK, h * LANES).astype(BF16), BF16, rope=True,
                          scale=LOG2E / math.sqrt(qk), heads=(h, LANES))],
                 tm=tm, seq=s, x_cols=(MLA_Q_RANK, 0), rms_gain=q_norm, rope=(MLA_ROPE // 2,) + tuple(tabs_mq))
    wkv3 = w_ukv.reshape(MLA_KV_RANK, h, MLA_NOPE + MLA_V)
    w_uk = wkv3[:, :, :MLA_NOPE].reshape(MLA_KV_RANK, h * MLA_NOPE).astype(BF16)
    w_uv = wkv3[:, :, MLA_NOPE:].reshape(MLA_KV_RANK, h * MLA_V).astype(BF16)
    k, v1 = _proj(c, [_Out(w_uk, BF16, heads=(h, MLA_NOPE), pad=("extra", LANES - MLA_NOPE)),
                      _Out(w_uv, BF16, heads=(h, MLA_V), pad=("ones", MLA_V))],
                  tm=tm, seq=s, x_cols=(MLA_KV_RANK, MLA_Q_RANK // MLA_KV_RANK), rms_gain=kv_norm, extra=kr)
    ft = min(FLASH_TQ, s)
    o = _flash(q.reshape(b * h, 1, s, LANES), k.reshape(b * h, s, LANES), v1.reshape(b * h, s, 2 * MLA_V),
               mode="causal", tq=ft, tk=ft, db=FLASH_DB, out_dtype=BF16)
    return o.reshape(b, h, s, MLA_V), xq


def _conv_mixer(x2, b, s, w_in, b_in, dw_w, dw_b, ln_g, ln_b, *, tm, ts):
    a, xq = _proj(x2, [_Out(w_in[:, :2 * CONV_CH].astype(BF16), F32),
                       _Out(w_in[:, 2 * CONV_CH:].astype(BF16), BF16, scale=XA_SCALE, heads=(XA_HEADS, HEAD_DIM))],
                  tm=tm, seq=s)
    mix = _conv(a.reshape(b, s, 2 * CONV_CH), b_in, dw_w, dw_b, ln_g, ln_b, ts=ts)
    return mix.reshape(b * s, CONV_CH), xq


def kernel(x, mem, positions, nsa_w_in, nsa_cmp_pe, nsa_cmp_w1, nsa_cmp_w2, mla_w_in, mla_q_norm, mla_w_uq, mla_kv_norm, mla_w_ukv, conv_w_in, conv_b_in, conv_dw_w, conv_dw_b, conv_ln_g, conv_ln_b, mem_w_kv, w_out, ln_g, ln_b, moe_w_grp, moe_b_grp, moe_w_exp, moe_b_exp, moe_w_gate, moe_w_up, moe_w_down):
    b, s, d = x.shape
    depth = w_out.shape[0]
    n_mem = mem.shape[1]
    alpha = (2.0 * depth) ** 0.25
    tm = 512
    tq = 512
    tabs_n = _rope_tables(positions, ROT_DIM, ROPE_THETA, HEAD_DIM)
    tabs_m = _rope_tables(positions, MLA_ROPE, MLA_THETA, MLA_ROPE)
    tabs_mq = _rope_tables(positions, MLA_ROPE, MLA_THETA, LANES, offset=MLA_NOPE)
    x2 = x.reshape(b * s, d)
    mem2 = mem.reshape(b * n_mem, d)
    for i in range(depth):
        kind, j = i % N_MIXERS, i // N_MIXERS
        if kind == 0:
            mix, xq = _nsa_mixer(x2, b, s, nsa_w_in[j], nsa_cmp_pe[j], nsa_cmp_w1[j], nsa_cmp_w2[j],
                                 tabs_n, tm=tm, tq=tq)
        elif kind == 1:
            mix, xq = _mla_mixer(x2, b, s, mla_w_in[j], mla_q_norm[j], mla_w_uq[j], mla_kv_norm[j],
                                 mla_w_ukv[j], tabs_m, tabs_mq, tm=tm, tq=tq)
        else:
            mix, xq = _conv_mixer(x2, b, s, conv_w_in[j], conv_b_in[j], conv_dw_w[j], conv_dw_b[j],
                                  conv_ln_g[j], conv_ln_b[j], tm=tm, ts=tq)
        xah = (XA_HEADS, HEAD_DIM)
        mk, mv1 = _proj(mem2, [_Out(mem_w_kv[i][:, :XA_WIDTH].astype(BF16), BF16, heads=xah),
                               _Out(mem_w_kv[i][:, XA_WIDTH:].astype(BF16), BF16, heads=xah,
                                    pad=("ones", HEAD_DIM))], tm=tm, seq=n_mem)
        xa = _flash(xq.reshape(b * XA_HEADS, 1, s, HEAD_DIM), mk.reshape(b * XA_HEADS, n_mem, HEAD_DIM),
                    mv1.reshape(b * XA_HEADS, n_mem, 2 * HEAD_DIM), mode="full", tq=min(FLASH_TQ, s), tk=n_mem,
                    out_dtype=BF16).reshape(b, XA_HEADS, s, HEAD_DIM)
        w_route = jnp.zeros((d, LANES), F32).at[:, :N_GROUPS].set(moe_w_grp[i])
        w_route = w_route.at[:, ROUTE_OFF:ROUTE_OFF + N_EXPERTS].set(moe_w_exp[i])
        b_route = jnp.zeros((1, LANES), F32).at[0, :N_GROUPS].set(moe_b_grp[i])
        b_route = b_route.at[0, ROUTE_OFF:ROUTE_OFF + N_EXPERTS].set(moe_b_exp[i])
        x1, gate = _out_proj(mix, xa, x2, w_out[i], ln_g[i, 0], ln_b[i, 0], w_route, b_route,
                             tm=tm, seq=s, alpha=alpha)
        x2 = _moe(x1, gate, moe_w_gate[i], moe_w_up[i], moe_w_down[i], ln_g[i, 1], ln_b[i, 1], alpha=alpha)
    return x2.reshape(b, s, d)
```

```python
import functools
import math

import numpy as np
import jax
import jax.numpy as jnp
from jax import lax
from jax.experimental import pallas as pl
from jax.experimental.pallas import tpu as pltpu

F32 = jnp.float32
BF16 = jnp.bfloat16

D_MODEL = 1024
N_MIXERS = 3
HEAD_DIM = 64
MIX_HEADS = 12
MIX_WIDTH = MIX_HEADS * HEAD_DIM
XA_HEADS = 4
XA_WIDTH = XA_HEADS * HEAD_DIM
ROPE_THETA = 500000.0
ROT_DIM = HEAD_DIM // 4
LN_EPS = 1e-5
RMS_EPS = 1e-6
NSA_KV_HEADS = 4
NSA_GROUP = MIX_HEADS // NSA_KV_HEADS
NSA_KV_WIDTH = NSA_KV_HEADS * HEAD_DIM
CMP_LEN = 32
CMP_STRIDE = 16
CMP_HIDDEN = 128
SEL_BLOCK = 64
SEL_SHIFT = 6
SEL_TOPK = 16
WINDOW = 512
MLA_Q_RANK = 256
MLA_KV_RANK = 128
MLA_NOPE = 64
MLA_ROPE = 32
MLA_V = 64
MLA_THETA = 10000.0
CONV_CH = MIX_WIDTH
CONV_WIDTH = 31
N_GROUPS = 4
EXPERTS_PER_GROUP = 4
N_EXPERTS = N_GROUPS * EXPERTS_PER_GROUP
EXPERT_FF = 512

LANES = 128
SUBLANES = 8
MXU_N = 256
VMEM_LIMIT = 56 * 1024 * 1024
NEG = -1e30
ROUTE_OFF = N_GROUPS
LOG2E = math.log2(math.e)
XA_SCALE = LOG2E / math.sqrt(HEAD_DIM)
FLASH_TQ = 1024
FLASH_DB = 512


def _cparams(*sem):
    return pltpu.CompilerParams(dimension_semantics=sem, vmem_limit_bytes=VMEM_LIMIT)


def _dot(a, b):
    return jnp.dot(a, b, preferred_element_type=F32)


def _dot_nt(a, b):
    return lax.dot_general(a, b, (((1,), (1,)), ((), ())), preferred_element_type=F32)


def _layer_norm(z, g, b):
    mu = jnp.mean(z, -1, keepdims=True)
    zc = z - mu
    var = jnp.mean(zc * zc, -1, keepdims=True)
    return zc * lax.rsqrt(var + LN_EPS) * g + b


def _sigmoid(x):
    return 1.0 / (1.0 + jnp.exp(-x))


def _rope(acc, r, c_ref, s1_ref, s2_ref):
    n = acc.shape[1]
    rep = n // LANES
    c = jnp.tile(c_ref[...], (1, rep))
    s1 = jnp.tile(s1_ref[...], (1, rep))
    s2 = jnp.tile(s2_ref[...], (1, rep))
    return acc * c + pltpu.roll(acc, n - r, 1) * s1 + pltpu.roll(acc, r, 1) * s2


class _Out:
    def __init__(self, w, dtype, rope=False, scale=1.0, heads=None, pad=None):
        self.w, self.dtype, self.rope, self.scale, self.heads, self.pad = w, dtype, rope, scale, heads, pad


def _proj_kernel(*refs, outs, rms, rope_r, has_extra, tm, spt):
    it = iter(refs)
    x_ref = next(it)
    g_ref = next(it) if rms else None
    tabs = (next(it), next(it), next(it)) if rope_r else None
    extra_ref = next(it) if has_extra else None
    w_refs = [next(it) for _ in outs]
    o_refs = [next(it) for _ in outs]
    xv = x_ref[...]
    if rms:
        xf = xv.astype(F32)
        xf = xf * lax.rsqrt(jnp.mean(xf * xf, -1, keepdims=True) + RMS_EPS) * g_ref[...]
        xb = xf.astype(BF16)
    else:
        xb = xv.astype(BF16)

    def pad_tile(kind, width):
        if kind == "onehot":
            tok = (pl.program_id(0) % spt) * tm + lax.broadcasted_iota(jnp.int32, (tm, width), 0)
            blk = lax.broadcasted_iota(jnp.int32, (tm, width), 1)
            return jnp.where(jnp.right_shift(tok, SEL_SHIFT) == blk, 1.0, 0.0)
        if kind == "extra":
            return extra_ref[:, :width].astype(F32)
        return jnp.full((tm, width), 1.0 if kind == "ones" else 0.0, F32)

    for o, w_ref, o_ref in zip(outs, w_refs, o_refs):
        n = w_ref.shape[1]
        step = MXU_N if n % MXU_N == 0 else LANES
        pad = pad_tile(*o.pad) if o.pad else None
        for c0 in range(0, n, step):
            acc = _dot(xb, w_ref[:, c0:c0 + step])
            if o.rope:
                acc = _rope(acc, rope_r, *tabs)
            if o.scale != 1.0:
                acc = acc * o.scale
            if o.heads is None:
                o_ref[:, c0:c0 + step] = acc.astype(o_ref.dtype)
            else:
                hw = o.heads[1]
                for j in range(step // hw):
                    piece = acc[:, j * hw:(j + 1) * hw]
                    if pad is not None:
                        piece = jnp.concatenate([piece, pad], axis=1)
                    o_ref[0, c0 // hw + j] = piece.astype(o_ref.dtype)


def _proj(x, outs, *, tm, seq, x_cols=None, rms_gain=None, rope=None, extra=None):
    m = x.shape[0]
    tm = min(tm, seq)
    spt = seq // tm
    kdim, xj = x_cols if x_cols else (x.shape[1], 0)
    ins = [x]
    in_specs = [pl.BlockSpec((tm, kdim), lambda i: (i, xj))]
    if rms_gain is not None:
        ins.append(rms_gain.reshape(1, kdim).astype(F32))
        in_specs.append(pl.BlockSpec((1, kdim), lambda i: (0, 0)))
    rope_r = 0
    if rope is not None:
        rope_r = rope[0]
        for t in rope[1:]:
            ins.append(t)
            in_specs.append(pl.BlockSpec((tm, LANES), lambda i: (i, 0)))
    if extra is not None:
        ins.append(extra)
        in_specs.append(pl.BlockSpec((tm, LANES), lambda i: (i, 0)))
    for o in outs:
        ins.append(o.w)
        in_specs.append(pl.BlockSpec(o.w.shape, lambda i: (0, 0)))
    out_shape, out_specs = [], []
    for o in outs:
        if o.heads is None:
            out_shape.append(jax.ShapeDtypeStruct((m, o.w.shape[1]), o.dtype))
            out_specs.append(pl.BlockSpec((tm, o.w.shape[1]), lambda i: (i, 0)))
        else:
            nh, hw = o.heads
            assert nh * hw == o.w.shape[1]
            width = hw + (o.pad[1] if o.pad else 0)
            out_shape.append(jax.ShapeDtypeStruct((m // seq, nh, seq, width), o.dtype))
            out_specs.append(pl.BlockSpec((1, nh, tm, width), lambda i: (i // spt, 0, i % spt, 0)))
    kern = functools.partial(_proj_kernel, outs=tuple(outs), rms=rms_gain is not None, rope_r=rope_r,
                             has_extra=extra is not None, tm=tm, spt=spt)
    return pl.pallas_call(kern, grid=(m // tm,), in_specs=in_specs, out_specs=out_specs,
                          out_shape=out_shape, compiler_params=_cparams("parallel"))(*ins)


def _flash_kernel(q_ref, k_ref, v_ref, o_ref, m_ref, acc_ref, *, mode, g, tq, tk, db):
    i = pl.program_id(1)
    dv = o_ref.shape[-1]
    m_ref[...] = jnp.full(m_ref.shape, NEG, F32)
    acc_ref[...] = jnp.zeros(acc_ref.shape, F32)

    def block(h, r0, nr, kstart, nk, kind):
        kt = k_ref[0, pl.ds(kstart, nk), :]
        vt = v_ref[0, pl.ds(kstart, nk), :]
        s = _dot_nt(q_ref[0, h, r0:r0 + nr, :], kt)
        if kind is not None:
            row = lax.broadcasted_iota(jnp.int32, (nr, nk), 0)
            col = lax.broadcasted_iota(jnp.int32, (nr, nk), 1)
            s = jnp.where(col <= row if kind == "diag" else col > row, s, NEG)
        a0 = h * tq + r0
        m_prev = m_ref[a0:a0 + nr, :]
        m_new = jnp.maximum(m_prev, jnp.max(s, -1, keepdims=True))
        alpha = jnp.exp2(m_prev - m_new)
        p = jnp.exp2(s - jnp.tile(m_new, (1, nk // LANES)))
        acc_ref[a0:a0 + nr, :] = alpha * acc_ref[a0:a0 + nr, :] + _dot(p.astype(BF16), vt)
        m_ref[a0:a0 + nr, :] = m_new

    if mode == "full":
        for j in range(k_ref.shape[1] // tk):
            for h in range(g):
                block(h, 0, tq, j * tk, tk, None)
    elif mode == "window":
        t0 = pl.multiple_of(i * tq, tq)
        for h in range(g):
            block(h, 0, tq, t0, tq, "diag")

        @pl.when(i > 0)
        def _():
            for h in range(g):
                block(h, 0, tq, pl.multiple_of(t0 - tq, tq), tq, "prev")
    else:
        t0 = pl.multiple_of(i * tq, tq)
        for a in range(tq // db):
            for h in range(g):
                block(h, a * db, db, pl.multiple_of(t0 + a * db, db), db, "diag")
                for c in range(a):
                    block(h, a * db, db, pl.multiple_of(t0 + c * db, db), db, None)

        def body(j, carry):
            for h in range(g):
                block(h, 0, tq, pl.multiple_of(j * tk, tk), tk, None)
            return carry
        lax.fori_loop(0, i * (tq // tk), body, 0)

    acc = acc_ref[...]
    o = acc / pltpu.roll(acc, dv, 1)
    o_ref[0] = o[:, :dv].reshape(g, tq, dv).astype(o_ref.dtype)


def _flash(q, k, v_ones, *, mode, tq, tk, out_dtype, db=None):
    n, g, s, dk = q.shape
    sk, dv = v_ones.shape[1], v_ones.shape[2] // 2
    assert 2 * dv == LANES
    kern = functools.partial(_flash_kernel, mode=mode, g=g, tq=tq, tk=tk, db=db or tq)
    return pl.pallas_call(
        kern, grid=(n, s // tq),
        in_specs=[pl.BlockSpec((1, g, tq, dk), lambda b, i: (b, 0, i, 0)),
                  pl.BlockSpec((1, sk, dk), lambda b, i: (b, 0, 0)),
                  pl.BlockSpec((1, sk, 2 * dv), lambda b, i: (b, 0, 0))],
        out_specs=pl.BlockSpec((1, g, tq, dv), lambda b, i: (b, 0, i, 0)),
        out_shape=jax.ShapeDtypeStruct((n, g, s, dv), out_dtype),
        scratch_shapes=[pltpu.VMEM((g * tq, LANES), F32), pltpu.VMEM((g * tq, 2 * dv), F32)],
        compiler_params=_cparams("parallel", "arbitrary"))(q, k, v_ones)


def _cmp_kernel(x_ref, w1s_ref, pe_ref, w1_ref, w2_ref, c_ref, s1_ref, s2_ref, o_ref, *, rope):
    nch = x_ref.shape[1]
    ab = _dot(x_ref[0], w1s_ref[...])
    pe_term = _dot(pe_ref[...], w1_ref[...])[0:1, :]
    nxt = pltpu.roll(ab[:, CMP_HIDDEN:], nch - 1, 0)
    h = ab[:, :CMP_HIDDEN] + nxt + pe_term
    h = h * _sigmoid(h)
    o = _dot(h.astype(BF16), w2_ref[...])
    if rope:
        o = _rope(o, ROT_DIM // 2, c_ref.at[0], s1_ref.at[0], s2_ref.at[0])
    o_ref[0] = o[:, :HEAD_DIM].astype(o_ref.dtype)


def _compress(tc, pe, w1, w2, tabs, *, rope, kvh):
    n, nch, width = tc.shape
    half = CMP_STRIDE * HEAD_DIM
    w1b = w1.astype(BF16)
    w1s = jnp.concatenate([w1b[:half], w1b[half:]], axis=1)
    pe8 = jnp.zeros((8, CMP_LEN * HEAD_DIM), BF16).at[0].set(pe.reshape(-1).astype(BF16))
    w2p = jnp.zeros((CMP_HIDDEN, LANES), BF16).at[:, :HEAD_DIM].set(w2.astype(BF16))
    kern = functools.partial(_cmp_kernel, rope=rope)
    tab_spec = pl.BlockSpec((1, nch, LANES), lambda i: (i // kvh, 0, 0))
    return pl.pallas_call(
        kern, grid=(n,),
        in_specs=[pl.BlockSpec((1, nch, width), lambda i: (i, 0, 0)),
                  pl.BlockSpec(w1s.shape, lambda i: (0, 0)),
                  pl.BlockSpec(pe8.shape, lambda i: (0, 0)),
                  pl.BlockSpec(w1b.shape, lambda i: (0, 0)),
                  pl.BlockSpec(w2p.shape, lambda i: (0, 0)),
                  tab_spec, tab_spec, tab_spec],
        out_specs=pl.BlockSpec((1, nch, HEAD_DIM), lambda i: (i, 0, 0)),
        out_shape=jax.ShapeDtypeStruct((n, nch, HEAD_DIM), BF16),
        compiler_params=_cparams("parallel"))(tc, w1s, pe8, w1b, w2p, *tabs)


def _cmp_attn_kernel(q_ref, kc_ref, vc_ref, ovl_ref, oc_ref, qa_ref, *, g, tq, n_cmp, scale):
    i = pl.program_id(1)
    nch = kc_ref.shape[1]
    n_sel = ovl_ref.shape[0]
    kc = kc_ref[0]
    vc = vc_ref[0]
    t0 = i * tq
    t_lane = t0 + lax.broadcasted_iota(jnp.int32, (nch, tq), 1)
    n_sub = lax.broadcasted_iota(jnp.int32, (nch, tq), 0)
    vis_t = (n_sub < n_cmp) & (n_sub * CMP_STRIDE + (CMP_LEN - 1) <= t_lane)
    p_sum_t = jnp.zeros((nch, tq), F32)
    for h in range(g):
        qh = q_ref[0, h]
        st = jnp.where(vis_t, _dot_nt(kc, qh) * scale, NEG)
        mt = jnp.max(st, 0, keepdims=True)
        et = jnp.where(vis_t, jnp.exp(st - mt), 0.0)
        pt = et / jnp.maximum(jnp.sum(et, 0, keepdims=True), 1e-30)
        p_sum_t = p_sum_t + pt
        oc_ref[0, h] = lax.dot_general(pt.astype(BF16), vc, (((0,), (0,)), ((), ())),
                                       preferred_element_type=F32).astype(oc_ref.dtype)
    imp = jnp.dot(ovl_ref[...], p_sum_t, preferred_element_type=F32,
                  precision=lax.Precision.HIGHEST)
    t1 = t0 + lax.broadcasted_iota(jnp.int32, (n_sel, tq), 1)
    blk = lax.broadcasted_iota(jnp.int32, (n_sel, tq), 0)
    cur = jnp.right_shift(t1, SEL_SHIFT)
    valid = blk <= cur
    forced = (blk == 0) | (blk == cur) | (blk == cur - 1)
    imp = jnp.where(valid, jnp.where(forced, jnp.inf, imp), -jnp.inf)
    nv = n_sel // SUBLANES
    groups = [imp[v * SUBLANES:(v + 1) * SUBLANES] for v in range(nv)]
    sub = lax.broadcasted_iota(jnp.int32, (SUBLANES, tq), 0)

    def count_group(rb, ranks):
        ranks = list(ranks)
        for rr in range(SUBLANES):
            row = groups[rb][rr:rr + 1, :]
            for v in range(nv):
                if v > rb:
                    ahead = jnp.where(row >= groups[v], 1.0, 0.0)
                elif v < rb:
                    ahead = jnp.where(row > groups[v], 1.0, 0.0)
                else:
                    ahead = jnp.where(sub > rr, jnp.where(row >= groups[v], 1.0, 0.0),
                                      jnp.where(row > groups[v], 1.0, 0.0))
                ranks[v] = ranks[v] + ahead
        return tuple(ranks)

    last_group = jnp.right_shift(t0 + (tq - 1), SEL_SHIFT) // SUBLANES
    ranks = tuple(jnp.zeros((SUBLANES, tq), F32) for _ in range(nv))
    for rb in range(nv):
        ranks = lax.cond(rb <= last_group, functools.partial(count_group, rb), lambda rk: rk, ranks)
    rank = jnp.concatenate(ranks, axis=0)
    chosen_t = jnp.where((rank < float(min(SEL_TOPK, n_sel))) & valid, 1.0, 0.0).astype(BF16)
    eye = (lax.broadcasted_iota(jnp.int32, (tq, tq), 0)
           == lax.broadcasted_iota(jnp.int32, (tq, tq), 1))
    chosen = _dot_nt(jnp.where(eye, 1.0, 0.0).astype(BF16), chosen_t)
    bias = (chosen - 1.0) * (-NEG)
    for h in range(g):
        qs = q_ref[0, h].astype(F32) * (scale * LOG2E)
        qa_ref[0, h] = jnp.concatenate([qs, bias], axis=1).astype(qa_ref.dtype)


def _cmp_attn(q, kc, vc, ovl_t, *, tq, n_cmp):
    n, g, s, d = q.shape
    nch = kc.shape[1]
    n_sel = ovl_t.shape[0]
    kern = functools.partial(_cmp_attn_kernel, g=g, tq=tq, n_cmp=n_cmp, scale=1.0 / math.sqrt(HEAD_DIM))
    return pl.pallas_call(
        kern, grid=(n, s // tq),
        in_specs=[pl.BlockSpec((1, g, tq, d), lambda b, i: (b, 0, i, 0)),
                  pl.BlockSpec((1, nch, d), lambda b, i: (b, 0, 0)),
                  pl.BlockSpec((1, nch, d), lambda b, i: (b, 0, 0)),
                  pl.BlockSpec(ovl_t.shape, lambda b, i: (0, 0))],
        out_specs=[pl.BlockSpec((1, g, tq, d), lambda b, i: (b, 0, i, 0)),
                   pl.BlockSpec((1, g, tq, d + n_sel), lambda b, i: (b, 0, i, 0))],
        out_shape=[jax.ShapeDtypeStruct((n, g, s, d), F32),
                   jax.ShapeDtypeStruct((n, g, s, d + n_sel), BF16)],
        compiler_params=_cparams("parallel", "arbitrary"))(q, kc, vc, ovl_t)


def _combine_kernel(gl_ref, oc_ref, os_ref, ow_ref, o_ref, *, nh):
    gates = _sigmoid(gl_ref[...])
    for h in range(nh):
        o = (gates[:, 3 * h:3 * h + 1] * oc_ref[0, h] + gates[:, 3 * h + 1:3 * h + 2] * os_ref[0, h]
             + gates[:, 3 * h + 2:3 * h + 3] * ow_ref[0, h])
        o_ref[0, h] = o.astype(o_ref.dtype)


def _combine(gl, oc, osel, ow, *, tq):
    bsz, nh, s, d = oc.shape
    spt = s // tq
    spec = pl.BlockSpec((1, nh, tq, d), lambda b, i: (b, 0, i, 0))
    return pl.pallas_call(
        functools.partial(_combine_kernel, nh=nh), grid=(bsz, spt),
        in_specs=[pl.BlockSpec((tq, LANES), lambda b, i: (b * spt + i, 0)), spec, spec, spec],
        out_specs=spec, out_shape=jax.ShapeDtypeStruct((bsz, nh, s, d), BF16),
        compiler_params=_cparams("parallel", "parallel"))(gl, oc, osel, ow)


CONV_HALO = 32
CONV_ROWS = 64


def _conv_kernel(cur_ref, prev_ref, bin_ref, w_ref, wb_ref, g_ref, b_ref, o_ref, u_ref, al_ref, *, ts):
    i = pl.program_id(1)
    ch = o_ref.shape[-1]

    def glu(a):
        a = a + bin_ref[...]
        return a[:, :ch] * _sigmoid(a[:, ch:])

    u_ref[CONV_HALO:CONV_HALO + ts, :] = glu(cur_ref[0])

    @pl.when(i == 0)
    def _():
        u_ref[0:CONV_HALO, :] = jnp.zeros((CONV_HALO, ch), F32)

    @pl.when(i > 0)
    def _():
        u_ref[0:CONV_HALO, :] = glu(prev_ref[0])

    base = CONV_HALO - (CONV_WIDTH - 1)
    for r0 in range(0, ts, CONV_ROWS):
        acc = jnp.zeros((CONV_ROWS, ch), F32)
        for sft in range(SUBLANES):
            taps = [k for k in range(CONV_WIDTH) if (base + k) % SUBLANES == sft]
            lo = base + taps[0]
            span = taps[-1] - taps[0] + CONV_ROWS
            al_ref[0:span, :] = u_ref[r0 + lo:r0 + lo + span, :]
            for k in taps:
                off = base + k - lo
                acc = acc + al_ref[off:off + CONV_ROWS, :] * w_ref[k:k + 1, :]
        y = _layer_norm(acc + wb_ref[...], g_ref[...], b_ref[...])
        o_ref[0, r0:r0 + CONV_ROWS, :] = (y * _sigmoid(y)).astype(o_ref.dtype)


def _conv(a, b_in, dw_w, dw_b, ln_g, ln_b, *, ts):
    bsz, s, two_ch = a.shape
    ch = two_ch // 2
    hb = ts // CONV_HALO
    row = lambda v: v.reshape(1, -1).astype(F32)
    return pl.pallas_call(
        functools.partial(_conv_kernel, ts=ts), grid=(bsz, s // ts),
        in_specs=[pl.BlockSpec((1, ts, two_ch), lambda b, i: (b, i, 0)),
                  pl.BlockSpec((1, CONV_HALO, two_ch), lambda b, i: (b, jnp.maximum(i * hb - 1, 0), 0)),
                  pl.BlockSpec((1, two_ch), lambda b, i: (0, 0)),
                  pl.BlockSpec((CONV_WIDTH, ch), lambda b, i: (0, 0)),
                  pl.BlockSpec((1, ch), lambda b, i: (0, 0)),
                  pl.BlockSpec((1, ch), lambda b, i: (0, 0)),
                  pl.BlockSpec((1, ch), lambda b, i: (0, 0))],
        out_specs=pl.BlockSpec((1, ts, ch), lambda b, i: (b, i, 0)),
        out_shape=jax.ShapeDtypeStruct((bsz, s, ch), BF16),
        scratch_shapes=[pltpu.VMEM((CONV_HALO + ts, ch), F32), pltpu.VMEM((CONV_HALO + CONV_ROWS, ch), F32)],
        compiler_params=_cparams("parallel", "arbitrary"))(
            a, a, row(b_in), dw_w.astype(F32), row(dw_b), row(ln_g), row(ln_b))


def _out_kernel(mix_ref, xa_ref, x_ref, w_ref, g_ref, b_ref, wr_ref, br_ref, x1_ref, gate_ref,
                cat_ref, *, alpha, mix_heads):
    if mix_heads:
        for h in range(MIX_HEADS):
            cat_ref[:, h * HEAD_DIM:(h + 1) * HEAD_DIM] = mix_ref[0, h].astype(F32)
    else:
        cat_ref[:, :MIX_WIDTH] = mix_ref[...].astype(F32)
    for h in range(XA_HEADS):
        cat_ref[:, MIX_WIDTH + h * HEAD_DIM:MIX_WIDTH + (h + 1) * HEAD_DIM] = xa_ref[0, h].astype(F32)
    y = _dot(cat_ref[...].astype(BF16), w_ref[...])
    x1 = _layer_norm(alpha * x_ref[...] + y, g_ref[...], b_ref[...])
    x1_ref[...] = x1
    hi = x1.astype(BF16)
    lo = (x1 - hi.astype(F32)).astype(BF16)
    wr = wr_ref[...]
    w_hi = wr.astype(BF16)
    w_lo = (wr - w_hi.astype(F32)).astype(BF16)
    logits = ((_dot(hi, w_hi) + _dot(lo, w_hi)) + (_dot(hi, w_lo) + _dot(lo, w_lo)) + br_ref[...])
    lane = lax.broadcasted_iota(jnp.int32, logits.shape, 1)
    far = jnp.int32(LANES)
    lg = jnp.where(lane < N_GROUPS, logits, -jnp.inf)
    mg = jnp.max(lg, -1, keepdims=True)
    g_prob = 1.0 / jnp.sum(jnp.exp(lg - mg), -1, keepdims=True)
    g_sel = jnp.min(jnp.where(lg == mg, lane, far), -1, keepdims=True)
    lo = ROUTE_OFF + EXPERTS_PER_GROUP * g_sel
    le = jnp.where((lane >= lo) & (lane < lo + EXPERTS_PER_GROUP), logits, -jnp.inf)
    m1 = jnp.max(le, -1, keepdims=True)
    i1 = jnp.min(jnp.where(le == m1, lane, far), -1, keepdims=True)
    le2 = jnp.where(lane == i1, -jnp.inf, le)
    m2 = jnp.max(le2, -1, keepdims=True)
    i2 = jnp.min(jnp.where(le2 == m2, lane, far), -1, keepdims=True)
    e2 = jnp.exp(m2 - m1)
    p1 = g_prob / (1.0 + e2)
    gates = jnp.where(lane == i1, p1, jnp.where(lane == i2, p1 * e2, 0.0))
    gate_ref[...] = jnp.where(lane == 0, g_sel.astype(F32), gates)


def _out_proj(mix, xa, x, w_out, ln_g, ln_b, w_route, b_route, *, tm, seq, alpha):
    t, d = x.shape
    tm = min(tm, seq)
    spt = seq // tm
    mix_heads = mix.ndim == 4
    row = lambda v: v.reshape(1, -1).astype(F32)
    full = lambda a: pl.BlockSpec(a.shape, lambda i: (0, 0))
    heads = lambda nh: pl.BlockSpec((1, nh, tm, HEAD_DIM), lambda i: (i // spt, 0, i % spt, 0))
    ins = [mix, xa, x, w_out.astype(BF16), row(ln_g), row(ln_b), w_route, b_route]
    in_specs = [heads(MIX_HEADS) if mix_heads else pl.BlockSpec((tm, MIX_WIDTH), lambda i: (i, 0)),
                heads(XA_HEADS),
                pl.BlockSpec((tm, d), lambda i: (i, 0))] + [full(a) for a in ins[3:]]
    return pl.pallas_call(
        functools.partial(_out_kernel, alpha=alpha, mix_heads=mix_heads), grid=(t // tm,), in_specs=in_specs,
        out_specs=[pl.BlockSpec((tm, d), lambda i: (i, 0)), pl.BlockSpec((tm, LANES), lambda i: (i, 0))],
        out_shape=[jax.ShapeDtypeStruct((t, d), F32), jax.ShapeDtypeStruct((t, LANES), F32)],
        scratch_shapes=[pltpu.VMEM((tm, MIX_WIDTH + XA_WIDTH), F32)],
        compiler_params=_cparams("parallel"))(*ins)


MOE_CHUNK = 2048
MOE_TILE = 256
MOE_UNROLL = 8
MOE_ALIGN = 16


def _moe_kernel(order_ref, meta_ref, x_ref, gate_ref, wg_ref, wu_ref, wd_ref, g_ref, b_ref, o_ref,
                xs_ref, gs_ref, ys_ref, tmp_ref, *, alpha, ch, tile):
    e = pl.program_id(1)
    grp = e // EXPERTS_PER_GROUP
    k = e % EXPERTS_PER_GROUP
    start = meta_ref[0, 0, grp]
    n_rows = meta_ref[0, 0, N_GROUPS + grp]
    n_copy = (n_rows + (MOE_UNROLL - 1)) // MOE_UNROLL * MOE_UNROLL
    n_mm = (n_rows + (MOE_ALIGN - 1)) // MOE_ALIGN * MOE_ALIGN
    n_full = n_mm // tile
    tail_start = pl.multiple_of(jnp.maximum(n_mm - tile, 0), MOE_ALIGN)
    done_rows = n_full * tile

    def token(pos):
        return order_ref[0, 0, start + pos]

    @pl.when((pl.program_id(0) == 0) & (e == 0))
    def _():
        tmp_ref[...] = jnp.zeros(tmp_ref.shape, F32)
        gs_ref[...] = jnp.zeros(gs_ref.shape, F32)
        ys_ref[...] = jnp.zeros(ys_ref.shape, F32)

    @pl.when(k == 0)
    def _():
        def gather_tile(rt, carry):
            r0 = pl.multiple_of(rt * tile, tile)

            def rows(blk, c2):
                i0 = pl.multiple_of(blk * MOE_UNROLL, MOE_UNROLL)
                g0 = pl.multiple_of(r0 + i0, MOE_UNROLL)
                for u in range(MOE_UNROLL):
                    t = token(g0 + u)
                    tmp_ref[pl.ds(i0 + u, 1), :] = x_ref[pl.ds(t, 1), :]
                    gs_ref[pl.ds(g0 + u, 1), :] = gate_ref[pl.ds(t, 1), :]
                return c2
            lax.fori_loop(0, jnp.minimum(n_copy - r0, tile) // MOE_UNROLL, rows, 0)
            xs_ref[pl.ds(r0, tile), :] = tmp_ref[...].astype(BF16)
            return carry
        lax.fori_loop(0, (n_copy + (tile - 1)) // tile, gather_tile, 0)

    def expert_rows(r0, nr, tail):
        lane = lax.broadcasted_iota(jnp.int32, (nr, LANES), 1)
        xb = xs_ref[pl.ds(r0, nr), :]
        gcol = jnp.sum(jnp.where(lane == ROUTE_OFF + e, gs_ref[pl.ds(r0, nr), :], 0.0), -1, keepdims=True)
        if tail:
            fresh = r0 + lax.broadcasted_iota(jnp.int32, (nr, 1), 0) >= done_rows
            gcol = jnp.where(fresh, gcol, 0.0)
        hg = _dot(xb, wg_ref[0])
        h = hg * _sigmoid(hg) * _dot(xb, wu_ref[0]) * gcol
        y = _dot(h.astype(BF16), wd_ref[0])

        @pl.when(k == 0)
        def _():
            if tail:
                ys_ref[pl.ds(r0, nr), :] = jnp.where(fresh, y, ys_ref[pl.ds(r0, nr), :])
            else:
                ys_ref[pl.ds(r0, nr), :] = y

        @pl.when(k > 0)
        def _():
            ys_ref[pl.ds(r0, nr), :] += y

    def full_tile(rt, carry):
        expert_rows(pl.multiple_of(rt * tile, tile), tile, False)
        return carry
    lax.fori_loop(0, n_full, full_tile, 0)

    half = tile // 2
    rem = n_mm - done_rows
    short = (rem <= half) & (n_mm >= half)

    @pl.when((rem > 0) & short)
    def _():
        expert_rows(pl.multiple_of(n_mm - half, MOE_ALIGN), half, True)

    @pl.when((rem > 0) & jnp.logical_not(short))
    def _():
        expert_rows(tail_start, tile, True)

    @pl.when(k == EXPERTS_PER_GROUP - 1)
    def _():
        def rows(blk, c2):
            i0 = pl.multiple_of(blk * MOE_UNROLL, MOE_UNROLL)
            for u in range(MOE_UNROLL):
                o_ref[pl.ds(token(i0 + u), 1), :] = ys_ref[pl.ds(i0 + u, 1), :]
            return c2
        lax.fori_loop(0, n_copy // MOE_UNROLL, rows, 0)

    @pl.when(e == pl.num_programs(1) - 1)
    def _():
        for r0 in range(0, ch, tile):
            z = alpha * x_ref[r0:r0 + tile, :] + o_ref[r0:r0 + tile, :]
            o_ref[r0:r0 + tile, :] = _layer_norm(z, g_ref[...], b_ref[...])


def _moe(x1, gate, wg, wu, wd, ln_g, ln_b, *, alpha):
    t, d = x1.shape
    ne, _, ff = wg.shape
    ch = min(MOE_CHUNK, t)
    nc = t // ch
    gsel = gate[:, 0].astype(jnp.int32).reshape(nc, ch)
    order = jnp.argsort(gsel, axis=1, stable=True).astype(jnp.int32)
    order = jnp.concatenate([order, jnp.broadcast_to(order[:, -1:], (nc, MOE_TILE))], axis=1)
    counts = jnp.sum(gsel[:, :, None] == jnp.arange(N_GROUPS, dtype=jnp.int32), axis=1, dtype=jnp.int32)
    starts = jnp.cumsum(counts, axis=1, dtype=jnp.int32) - counts
    meta = jnp.concatenate([starts, counts], axis=1)
    row = lambda v: v.reshape(1, -1).astype(F32)
    smem = lambda n: pl.BlockSpec((1, 1, n), lambda c, e: (c, 0, 0), memory_space=pltpu.SMEM)
    return pl.pallas_call(
        functools.partial(_moe_kernel, alpha=alpha, ch=ch, tile=MOE_TILE), grid=(nc, ne),
        in_specs=[smem(ch + MOE_TILE), smem(2 * N_GROUPS),
                  pl.BlockSpec((ch, d), lambda c, e: (c, 0), pipeline_mode=pl.Buffered(1)),
                  pl.BlockSpec((ch, LANES), lambda c, e: (c, 0)),
                  pl.BlockSpec((1, d, ff), lambda c, e: (e, 0, 0)),
                  pl.BlockSpec((1, d, ff), lambda c, e: (e, 0, 0)),
                  pl.BlockSpec((1, ff, d), lambda c, e: (e, 0, 0)),
                  pl.BlockSpec((1, d), lambda c, e: (0, 0)),
                  pl.BlockSpec((1, d), lambda c, e: (0, 0))],
        out_specs=pl.BlockSpec((ch, d), lambda c, e: (c, 0)),
        out_shape=jax.ShapeDtypeStruct((t, d), F32),
        scratch_shapes=[pltpu.VMEM((ch, d), BF16), pltpu.VMEM((ch, LANES), F32), pltpu.VMEM((ch, d), F32),
                        pltpu.VMEM((MOE_TILE, d), F32)],
        compiler_params=_cparams("parallel", "arbitrary"))(
            order.reshape(nc, 1, ch + MOE_TILE), meta.reshape(nc, 1, 2 * N_GROUPS), x1, gate,
            wg.astype(BF16), wu.astype(BF16), wd.astype(BF16), row(ln_g), row(ln_b))


def _rope_tables(positions, dim, theta, period, offset=0):
    b, s = positions.shape
    half = dim // 2
    inv = theta ** (-jnp.arange(0, dim, 2, dtype=F32) / dim)
    ang = positions.astype(F32)[..., None] * inv
    cos, sin = jnp.cos(ang), jnp.sin(ang)
    rest = period - dim - offset
    fill = lambda v, n: jnp.full((b, s, n), v, F32)
    c = jnp.concatenate([fill(1.0, offset), cos, cos, fill(1.0, rest)], -1)
    s1 = jnp.concatenate([fill(0.0, offset), -sin, fill(0.0, half + rest)], -1)
    s2 = jnp.concatenate([fill(0.0, offset + half), sin, fill(0.0, rest)], -1)
    rep = LANES // period
    return tuple(jnp.tile(t, (1, 1, rep)).reshape(b * s, LANES) for t in (c, s1, s2))


def _nsa_mixer(x2, b, s, w_in, cmp_pe, cmp_w1, cmp_w2, tabs_n, *, tm, tq):
    kv, g, d = NSA_KV_HEADS, NSA_GROUP, HEAD_DIM
    offs = np.cumsum([0, MIX_WIDTH] + [NSA_KV_WIDTH] * 6 + [3 * MIX_HEADS, XA_WIDTH]).tolist()
    col = lambda k: w_in[:, offs[k]:offs[k + 1]]
    wq, wkc, wvc, wks, wvs, wkw, wvw, wgt, wxq = [col(k) for k in range(9)]
    w_gate = jnp.zeros((D_MODEL, LANES), F32).at[:, :3 * MIX_HEADS].set(wgt).astype(BF16)
    n_sel = s // SEL_BLOCK
    bf = lambda w: w.astype(BF16)
    kvh = (kv, d)
    q, k_aug, kw_aug, kc, vc, vs1, vw1, pg, xq = _proj(
        x2, [_Out(bf(wq), BF16, rope=True, heads=(MIX_HEADS, d)),
             _Out(bf(wks), BF16, rope=True, heads=kvh, pad=("onehot", n_sel)),
             _Out(bf(wkw), BF16, rope=True, heads=kvh, pad=("zeros", n_sel)),
             _Out(bf(wkc), BF16, heads=kvh), _Out(bf(wvc), BF16, heads=kvh),
             _Out(bf(wvs), BF16, heads=kvh, pad=("ones", d)), _Out(bf(wvw), BF16, heads=kvh, pad=("ones", d)),
             _Out(w_gate, F32), _Out(bf(wxq), BF16, scale=XA_SCALE, heads=(XA_HEADS, d))],
        tm=tm, seq=s, rope=(ROT_DIM // 2,) + tuple(tabs_n))
    q = q.reshape(b * kv, g, s, d)
    nch = s // CMP_STRIDE
    kc = kc.reshape(b * kv, nch, CMP_STRIDE * d)
    vc = vc.reshape(b * kv, nch, CMP_STRIDE * d)

    n_cmp = (s - CMP_LEN) // CMP_STRIDE + 1
    end_tabs = tuple(t.reshape(b, s, LANES)[:, CMP_LEN - 1::CMP_STRIDE][:, :nch] for t in tabs_n)
    end_tabs = tuple(jnp.pad(t, ((0, 0), (0, nch - t.shape[1]), (0, 0))) for t in end_tabs)
    k_cmp = _compress(kc, cmp_pe[0], cmp_w1[0], cmp_w2[0], end_tabs, rope=True, kvh=kv)
    v_cmp = _compress(vc, cmp_pe[1], cmp_w1[1], cmp_w2[1], end_tabs, rope=False, kvh=kv)

    starts = np.arange(nch) * CMP_STRIDE
    sel_start = np.arange(n_sel) * SEL_BLOCK
    ovl = ((starts[:, None] < sel_start[None, :] + SEL_BLOCK)
           & (starts[:, None] + CMP_LEN > sel_start[None, :])
           & (np.arange(nch)[:, None] < n_cmp)).astype(np.float32)
    o_c, q_aug = _cmp_attn(q, k_cmp, v_cmp, jnp.asarray(ovl.T), tq=tq, n_cmp=n_cmp)

    ft = min(FLASH_TQ, s)
    flat = lambda a: a.reshape(b * kv, s, a.shape[-1])
    o_s = _flash(q_aug, flat(k_aug), flat(vs1), mode="causal", tq=ft, tk=ft, db=FLASH_DB, out_dtype=F32)
    o_w = _flash(q_aug, flat(kw_aug), flat(vw1), mode="window", tq=WINDOW, tk=WINDOW, out_dtype=F32)
    hm = lambda a: a.reshape(b, MIX_HEADS, s, d)
    mix = _combine(pg, hm(o_c), hm(o_s), hm(o_w), tq=tq)
    return mix, xq


def _mla_mixer(x2, b, s, w_in, q_norm, w_uq, kv_norm, w_ukv, tabs_m, tabs_mq, *, tm, tq):
    h = MIX_HEADS
    qk = MLA_NOPE + MLA_ROPE
    w_c = w_in[:, :MLA_Q_RANK + MLA_KV_RANK].astype(BF16)
    w_kr = jnp.zeros((D_MODEL, LANES), F32).at[:, :MLA_ROPE].set(
        w_in[:, MLA_Q_RANK + MLA_KV_RANK:MLA_Q_RANK + MLA_KV_RANK + MLA_ROPE]).astype(BF16)
    w_xq = w_in[:, -XA_WIDTH:].astype(BF16)
    c, kr, xq = _proj(x2, [_Out(w_c, F32), _Out(w_kr, BF16, rope=True),
                           _Out(w_xq, BF16, scale=XA_SCALE, heads=(XA_HEADS, HEAD_DIM))],
                      tm=tm, seq=s, rope=(MLA_ROPE // 2,) + tuple(tabs_m))
    wq3 = jnp.pad(w_uq.reshape(MLA_Q_RANK, h, qk), ((0, 0), (0, 0), (0, LANES - qk)))
    (q,) = _proj(c, [_Out(wq3.reshape(MLA_Q_RANK, h * LANES).astype(BF16), BF16, rope=True,
                          scale=LOG2E / math.sqrt(qk), heads=(h, LANES))],
                 tm=tm, seq=s, x_cols=(MLA_Q_RANK, 0), rms_gain=q_norm, rope=(MLA_ROPE // 2,) + tuple(tabs_mq))
    wkv3 = w_ukv.reshape(MLA_KV_RANK, h, MLA_NOPE + MLA_V)
    w_uk = wkv3[:, :, :MLA_NOPE].reshape(MLA_KV_RANK, h * MLA_NOPE).astype(BF16)
    w_uv = wkv3[:, :, MLA_NOPE:].reshape(MLA_KV_RANK, h * MLA_V).astype(BF16)
    k, v1 = _proj(c, [_Out(w_uk, BF16, heads=(h, MLA_NOPE), pad=("extra", LANES - MLA_NOPE)),
                      _Out(w_uv, BF16, heads=(h, MLA_V), pad=("ones", MLA_V))],
                  tm=tm, seq=s, x_cols=(MLA_KV_RANK, MLA_Q_RANK // MLA_KV_RANK), rms_gain=kv_norm, extra=kr)
    ft = min(FLASH_TQ, s)
    o = _flash(q.reshape(b * h, 1, s, LANES), k.reshape(b * h, s, LANES), v1.reshape(b * h, s, 2 * MLA_V),
               mode="causal", tq=ft, tk=ft, db=FLASH_DB, out_dtype=BF16)
    return o.reshape(b, h, s, MLA_V), xq


def _conv_mixer(x2, b, s, w_in, b_in, dw_w, dw_b, ln_g, ln_b, *, tm, ts):
    a, xq = _proj(x2, [_Out(w_in[:, :2 * CONV_CH].astype(BF16), F32),
                       _Out(w_in[:, 2 * CONV_CH:].astype(BF16), BF16, scale=XA_SCALE, heads=(XA_HEADS, HEAD_DIM))],
                  tm=tm, seq=s)
    mix = _conv(a.reshape(b, s, 2 * CONV_CH), b_in, dw_w, dw_b, ln_g, ln_b, ts=ts)
    return mix.reshape(b * s, CONV_CH), xq


def kernel(x, mem, positions, nsa_w_in, nsa_cmp_pe, nsa_cmp_w1, nsa_cmp_w2, mla_w_in, mla_q_norm, mla_w_uq, mla_kv_norm, mla_w_ukv, conv_w_in, conv_b_in, conv_dw_w, conv_dw_b, conv_ln_g, conv_ln_b, mem_w_kv, w_out, ln_g, ln_b, moe_w_grp, moe_b_grp, moe_w_exp, moe_b_exp, moe_w_gate, moe_w_up, moe_w_down):
    b, s, d = x.shape
    depth = w_out.shape[0]
    n_mem = mem.shape[1]
    alpha = (2.0 * depth) ** 0.25
    tm = 512
    tq = 512
    tabs_n = _rope_tables(positions, ROT_DIM, ROPE_THETA, HEAD_DIM)
    tabs_m = _rope_tables(positions, MLA_ROPE, MLA_THETA, MLA_ROPE)
    tabs_mq = _rope_tables(positions, MLA_ROPE, MLA_THETA, LANES, offset=MLA_NOPE)
    x2 = x.reshape(b * s, d)
    mem2 = mem.reshape(b * n_mem, d)
    for i in range(depth):
        kind, j = i % N_MIXERS, i // N_MIXERS
        if kind == 0:
            mix, xq = _nsa_mixer(x2, b, s, nsa_w_in[j], nsa_cmp_pe[j], nsa_cmp_w1[j], nsa_cmp_w2[j],
                                 tabs_n, tm=tm, tq=tq)
        elif kind == 1:
            mix, xq = _mla_mixer(x2, b, s, mla_w_in[j], mla_q_norm[j], mla_w_uq[j], mla_kv_norm[j],
                                 mla_w_ukv[j], tabs_m, tabs_mq, tm=tm, tq=tq)
        else:
            mix, xq = _conv_mixer(x2, b, s, conv_w_in[j], conv_b_in[j], conv_dw_w[j], conv_dw_b[j],
                                  conv_ln_g[j], conv_ln_b[j], tm=tm, ts=tq)
        xah = (XA_HEADS, HEAD_DIM)
        mk, mv1 = _proj(mem2, [_Out(mem_w_kv[i][:, :XA_WIDTH].astype(BF16), BF16, heads=xah),
                               _Out(mem_w_kv[i][:, XA_WIDTH:].astype(BF16), BF16, heads=xah,
                                    pad=("ones", HEAD_DIM))], tm=tm, seq=n_mem)
        xa = _flash(xq.reshape(b * XA_HEADS, 1, s, HEAD_DIM), mk.reshape(b * XA_HEADS, n_mem, HEAD_DIM),
                    mv1.reshape(b * XA_HEADS, n_mem, 2 * HEAD_DIM), mode="full", tq=min(FLASH_TQ, s), tk=n_mem,
                    out_dtype=BF16).reshape(b, XA_HEADS, s, HEAD_DIM)
        w_route = jnp.zeros((d, LANES), F32).at[:, :N_GROUPS].set(moe_w_grp[i])
        w_route = w_route.at[:, ROUTE_OFF:ROUTE_OFF + N_EXPERTS].set(moe_w_exp[i])
        b_route = jnp.zeros((1, LANES), F32).at[0, :N_GROUPS].set(moe_b_grp[i])
        b_route = b_route.at[0, ROUTE_OFF:ROUTE_OFF + N_EXPERTS].set(moe_b_exp[i])
        x1, gate = _out_proj(mix, xa, x2, w_out[i], ln_g[i, 0], ln_b[i, 0], w_route, b_route,
                             tm=tm, seq=s, alpha=alpha)
        x2 = _moe(x1, gate, moe_w_gate[i], moe_w_up[i], moe_w_down[i], ln_g[i, 1], ln_b[i, 1], alpha=alpha)
    return x2.reshape(b, s, d)
```
